```python
import jax, jax.numpy as jnp
from jax import lax
import numpy as np

D_MODEL = 1024
BATCH = 8
SEQ = 4096
DEPTH = 2

HEAD_DIM = 64
ATTN_Q_HEADS = 8
ATTN_KV_HEADS = 2
ATTN_GROUP = ATTN_Q_HEADS // ATTN_KV_HEADS
WINDOW = 128
ATTN_BLOCK = 128
ROPE_THETA = 10000.0
ATTN_WIDTH = ATTN_Q_HEADS * HEAD_DIM
KV_WIDTH = ATTN_KV_HEADS * HEAD_DIM
CONV_WIDTH = D_MODEL // 2
SHORT_CONV_K = 3
HY_SPLITS = (ATTN_WIDTH, KV_WIDTH, KV_WIDTH, CONV_WIDTH, CONV_WIDTH, CONV_WIDTH)
HY_IN_WIDTH = sum(HY_SPLITS)
HY_OUT_WIDTH = ATTN_WIDTH + CONV_WIDTH
GLA_HEADS = 4
GLA_KEY_DIM = D_MODEL // 2
GLA_VAL_DIM = D_MODEL
GLA_DK = GLA_KEY_DIM // GLA_HEADS
GLA_DV = GLA_VAL_DIM // GLA_HEADS
GLA_GATE_RANK = 16
GLA_GATE_NORMALIZER = 16.0
GLA_CHUNK = 64
GLA_SPLITS = (GLA_KEY_DIM, GLA_KEY_DIM, GLA_VAL_DIM, GLA_GATE_RANK, GLA_VAL_DIM)
GLA_IN_WIDTH = sum(GLA_SPLITS)
D_FF = 2816
FFN_CONV_K = 3
PLE_DIM = 256
MAX_POS_OFFSET = 1024
N_EVEN = (DEPTH + 1) // 2
N_ODD = DEPTH // 2
RMS_EPS = 1e-6

kernel_name = 'hybrid_swa_shortconv_gla_convffn'


def _split(z, widths):
    idx, acc = [], 0
    for w in widths[:-1]:
        acc += w
        idx.append(acc)
    return jnp.split(z, idx, axis=-1)


def _rms_norm(x, gain):
    xf = x.astype(jnp.float32)
    y = xf * lax.rsqrt(jnp.mean(xf * xf, axis=-1, keepdims=True) + RMS_EPS)
    return (y * gain.astype(jnp.float32)).astype(x.dtype)


def _causal_depthwise_conv(x, w):
    k = w.shape[0]
    return lax.conv_general_dilated(
        x, w[:, None, :].astype(x.dtype), window_strides=(1,), padding=[(k - 1, 0)],
        dimension_numbers=('NWC', 'WIO', 'NWC'), feature_group_count=x.shape[-1])


def _rope_tables(positions):
    inv_freq = ROPE_THETA ** (-jnp.arange(0, HEAD_DIM, 2, dtype=jnp.float32) / HEAD_DIM)
    ang = positions.astype(jnp.float32)[..., None] * inv_freq
    return jnp.cos(ang)[:, :, None, :], jnp.sin(ang)[:, :, None, :]


def _apply_rope(x, cos, sin):
    x1, x2 = jnp.split(x.astype(jnp.float32), 2, axis=-1)
    return jnp.concatenate([x1 * cos - x2 * sin, x2 * cos + x1 * sin], axis=-1).astype(x.dtype)


def _sliding_window_sink_attention(q, k, v, sinks):
    b, s = q.shape[0], q.shape[1]
    nb = s // ATTN_BLOCK
    qb = q.reshape(b, nb, ATTN_BLOCK, ATTN_KV_HEADS, ATTN_GROUP, HEAD_DIM)

    def band(t):
        tb = t.reshape(b, nb, ATTN_BLOCK, ATTN_KV_HEADS, HEAD_DIM)
        prev = jnp.pad(tb[:, :-1], ((0, 0), (1, 0), (0, 0), (0, 0), (0, 0)))
        return jnp.concatenate([prev, tb], axis=2)

    kband, vband = band(k), band(v)
    scores = jnp.einsum('bnqhgd,bnkhd->bnhgqk', qb, kband).astype(jnp.float32) * (HEAD_DIM ** -0.5)
    blk = jnp.arange(nb)[:, None] * ATTN_BLOCK
    q_pos = blk + jnp.arange(ATTN_BLOCK)[None, :]
    k_pos = blk - ATTN_BLOCK + jnp.arange(2 * ATTN_BLOCK)[None, :]
    dist = q_pos[:, :, None] - k_pos[:, None, :]
    allowed = (dist >= 0) & (dist < WINDOW) & (k_pos[:, None, :] >= 0)
    scores = jnp.where(allowed[None, :, None, None], scores, -jnp.inf)
    sink = jnp.broadcast_to(
        sinks.astype(jnp.float32).reshape(1, 1, ATTN_KV_HEADS, ATTN_GROUP, 1, 1), scores.shape[:-1] + (1,))
    probs = jax.nn.softmax(jnp.concatenate([scores, sink], axis=-1), axis=-1)[..., :-1]
    out = jnp.einsum('bnhgqk,bnkhd->bnqhgd', probs.astype(v.dtype), vband)
    return out.reshape(b, s, ATTN_WIDTH)


def _attn_conv_mixer(h, cos, sin, w_in, q_norm, k_norm, sinks, conv_w, w_out):
    b, s, _ = h.shape
    q, k, v, cb, cc, cx = _split(h @ w_in, HY_SPLITS)
    q = _apply_rope(_rms_norm(q.reshape(b, s, ATTN_Q_HEADS, HEAD_DIM), q_norm), cos, sin)
    k = _apply_rope(_rms_norm(k.reshape(b, s, ATTN_KV_HEADS, HEAD_DIM), k_norm), cos, sin)
    v = v.reshape(b, s, ATTN_KV_HEADS, HEAD_DIM)
    attn = _sliding_window_sink_attention(q, k, v, sinks)
    conv = cb * _causal_depthwise_conv(cc * cx, conv_w)
    return jnp.concatenate([attn, conv], axis=-1) @ w_out


def _gla_chunked(q, k, v, log_a):
    b, s, nh, dk = q.shape
    dv = v.shape[-1]
    nc = s // GLA_CHUNK

    def chunks(t):
        return t.astype(jnp.float32).reshape(b, nc, GLA_CHUNK, nh, t.shape[-1]).transpose(1, 0, 3, 2, 4)

    qc, kc, vc = chunks(q), chunks(k), chunks(v)
    gc = jnp.cumsum(chunks(log_a), axis=3)
    causal = jnp.tril(jnp.ones((GLA_CHUNK, GLA_CHUNK), dtype=bool))

    def step(state, inp):
        qi, ki, vi, gi = inp
        g_end = gi[:, :, -1, :]
        o_inter = jnp.einsum('bhtd,bhdv->bhtv', qi * jnp.exp(gi), state)
        rel = gi[:, :, :, None, :] - gi[:, :, None, :, :]
        decay = jnp.exp(jnp.where(causal[:, :, None], rel, -jnp.inf))
        scores = jnp.einsum('bhtd,bhsd,bhtsd->bhts', qi, ki, decay)
        o_intra = jnp.einsum('bhts,bhsv->bhtv', scores, vi)
        state = jnp.exp(g_end)[..., None] * state + jnp.einsum(
            'bhsd,bhsv->bhdv', ki * jnp.exp(g_end[:, :, None, :] - gi), vi)
        return state, o_inter + o_intra

    state0 = jnp.zeros((b, nh, dk, dv), jnp.float32)
    _, o = lax.scan(step, state0, (qc, kc, vc, gc))
    return o.transpose(1, 0, 3, 2, 4).reshape(b, s, nh, dv)


def _gla_mixer(h, w_in, w_gate_up, gate_bias, o_norm, w_out):
    b, s, _ = h.shape
    q, k, v, g_lr, og = _split(h @ w_in, GLA_SPLITS)
    log_a = jax.nn.log_sigmoid((g_lr @ w_gate_up + gate_bias).astype(jnp.float32)) / GLA_GATE_NORMALIZER

    def heads(t, d):
        return t.reshape(b, s, GLA_HEADS, d)

    o = _gla_chunked(heads(q, GLA_DK) * (GLA_DK ** -0.5), heads(k, GLA_DK), heads(v, GLA_DV), heads(log_a, GLA_DK))
    o = _rms_norm(o.astype(h.dtype), o_norm) * jax.nn.silu(heads(og, GLA_DV))
    return o.reshape(b, s, GLA_VAL_DIM) @ w_out


def _conv_ffn(h, w_up, conv_w, conv_b, w_down):
    u = _causal_depthwise_conv(h @ w_up, conv_w) + conv_b
    gate, up = jnp.split(u, 2, axis=-1)
    return (jax.nn.gelu(gate, approximate=False) * up) @ w_down


def _per_layer_embedding(x, p_i, norm, w_gate, w_proj):
    gate = jax.nn.sigmoid(_rms_norm(x, norm) @ w_gate)
    return gate * (p_i @ w_proj)


def setup_inputs(seed: int = 0) -> dict:
    key = jax.random.key(seed)
    ks = jax.random.split(key, 24)
    f32 = jnp.float32

    def normal(k, shape, fan_in):
        return jax.random.normal(k, shape, f32) * (fan_in ** -0.5)

    def gain(k, shape):
        return 1.0 + 0.05 * jax.random.normal(k, shape, f32)

    x = jax.random.normal(ks[0], (BATCH, SEQ, D_MODEL), f32)
    p = jax.random.normal(ks[1], (DEPTH, BATCH, SEQ, PLE_DIM), f32)
    offset = jax.random.randint(ks[2], (BATCH, 1), 0, MAX_POS_OFFSET, dtype=jnp.int32)
    positions = offset + jnp.arange(SEQ, dtype=jnp.int32)[None, :]
    return {
        'x': x,
        'p': p,
        'positions': positions,
        'mix_norm': gain(ks[3], (DEPTH, D_MODEL)),
        'ffn_norm': gain(ks[4], (DEPTH, D_MODEL)),
        'ffn_w_up': normal(ks[5], (DEPTH, D_MODEL, 2 * D_FF), D_MODEL),
        'ffn_conv_w': normal(ks[6], (DEPTH, FFN_CONV_K, 2 * D_FF), FFN_CONV_K),
        'ffn_conv_b': 0.02 * jax.random.normal(ks[7], (DEPTH, 2 * D_FF), f32),
        'ffn_w_down': normal(ks[8], (DEPTH, D_FF, D_MODEL), D_FF),
        'ple_norm': gain(ks[9], (DEPTH, D_MODEL)),
        'ple_w_gate': normal(ks[10], (DEPTH, D_MODEL, D_MODEL), D_MODEL),
        'ple_w_proj': normal(ks[11], (DEPTH, PLE_DIM, D_MODEL), PLE_DIM),
        'hy_w_in': normal(ks[12], (N_EVEN, D_MODEL, HY_IN_WIDTH), D_MODEL),
        'hy_q_norm': gain(ks[13], (N_EVEN, HEAD_DIM)),
        'hy_k_norm': gain(ks[14], (N_EVEN, HEAD_DIM)),
        'hy_sinks': jax.random.normal(ks[15], (N_EVEN, ATTN_Q_HEADS), f32),
        'hy_conv_w': normal(ks[16], (N_EVEN, SHORT_CONV_K, CONV_WIDTH), SHORT_CONV_K),
        'hy_w_out': normal(ks[17], (N_EVEN, HY_OUT_WIDTH, D_MODEL), HY_OUT_WIDTH),
        'gla_w_in': normal(ks[18], (N_ODD, D_MODEL, GLA_IN_WIDTH), D_MODEL),
        'gla_w_gate_up': normal(ks[19], (N_ODD, GLA_GATE_RANK, GLA_KEY_DIM), GLA_GATE_RANK),
        'gla_gate_bias': 0.1 * jax.random.normal(ks[20], (N_ODD, GLA_KEY_DIM), f32),
        'gla_o_norm': gain(ks[21], (N_ODD, GLA_DV)),
        'gla_w_out': normal(ks[22], (N_ODD, GLA_VAL_DIM, D_MODEL), GLA_VAL_DIM),
    }


def reference(x, p, positions, mix_norm, ffn_norm, ffn_w_up, ffn_conv_w, ffn_conv_b, ffn_w_down,
              ple_norm, ple_w_gate, ple_w_proj, hy_w_in, hy_q_norm, hy_k_norm, hy_sinks, hy_conv_w,
              hy_w_out, gla_w_in, gla_w_gate_up, gla_gate_bias, gla_o_norm, gla_w_out):
    cos, sin = _rope_tables(positions)
    for i in range(DEPTH):
        j = i // 2
        h = _rms_norm(x, mix_norm[i])
        if i % 2 == 0:
            x = x + _attn_conv_mixer(h, cos, sin, hy_w_in[j], hy_q_norm[j], hy_k_norm[j], hy_sinks[j],
                                     hy_conv_w[j], hy_w_out[j])
        else:
            x = x + _gla_mixer(h, gla_w_in[j], gla_w_gate_up[j], gla_gate_bias[j], gla_o_norm[j], gla_w_out[j])
        x = x + _conv_ffn(_rms_norm(x, ffn_norm[i]), ffn_w_up[i], ffn_conv_w[i], ffn_conv_b[i], ffn_w_down[i])
        x = x + _per_layer_embedding(x, p[i], ple_norm[i], ple_w_gate[i], ple_w_proj[i])
    return x
```

```python
import functools

import jax
import jax.numpy as jnp
from jax import lax
from jax.experimental import pallas as pl
from jax.experimental.pallas import tpu as pltpu

D_MODEL = 1024
HEAD_DIM = 64
ATTN_Q_HEADS = 8
ATTN_KV_HEADS = 2
ATTN_BLOCK = 128
ROPE_THETA = 10000.0
ATTN_WIDTH = ATTN_Q_HEADS * HEAD_DIM
KV_WIDTH = ATTN_KV_HEADS * HEAD_DIM
CONV_WIDTH = D_MODEL // 2
HY_IN_WIDTH = ATTN_WIDTH + 2 * KV_WIDTH + 3 * CONV_WIDTH
GLA_HEADS = 4
GLA_KEY_DIM = D_MODEL // 2
GLA_VAL_DIM = D_MODEL
GLA_DK = GLA_KEY_DIM // GLA_HEADS
GLA_DV = GLA_VAL_DIM // GLA_HEADS
GLA_GATE_RANK = 16
GLA_GATE_NORMALIZER = 16.0
GLA_CHUNK = 64
GLA_SUB = 16
D_FF = 2816
PLE_DIM = 256
RMS_EPS = 1e-6

LANES = 128
SUBLANES = 8
SEQ_TILE = 512
FF_CHUNK = 256
VMEM_LIMIT = 56 * 1024 * 1024
NEG_BIG = -1e30

F32 = jnp.float32
BF16 = jnp.bfloat16


def _dot(a, b):
    return jnp.dot(a, b, preferred_element_type=F32)


def _dot_nt(a, b):
    return lax.dot_general(a, b, (((1,), (1,)), ((), ())), preferred_element_type=F32)


def _dot_tn(a, b):
    return lax.dot_general(a, b, (((0,), (0,)), ((), ())), preferred_element_type=F32)


def _rms_norm(x, gain):
    return x * lax.rsqrt(jnp.mean(x * x, axis=-1, keepdims=True) + RMS_EPS) * gain


def _shift_rows(u, tail, k):
    n = u.shape[0]
    ext = jnp.concatenate([tail, u], axis=0)
    return ext[SUBLANES - k:SUBLANES - k + n]


def _hy_mixer_kernel(x_ref, pos_ref, invf_ref, norm_ref, w_in_ref, qn_ref, kn_ref, sink_ref, hsum_ref,
                     conv_w_ref, w_out_ref, o_ref, k_tail, v_tail, y_tail, attn_buf):
    ti = pl.program_id(1)
    ts = x_ref.shape[1]

    @pl.when(ti == 0)
    def _():
        k_tail[...] = jnp.zeros_like(k_tail)
        v_tail[...] = jnp.zeros_like(v_tail)
        y_tail[...] = jnp.zeros_like(y_tail)

    x = x_ref[0]
    h = _rms_norm(x, norm_ref[...]).astype(BF16)

    pos = pos_ref[0].astype(F32)
    ang = invf_ref[...][:, 0:1] * pos
    cos_t = jnp.cos(ang).T
    sin_t = jnp.sin(ang).T
    lane = lax.broadcasted_iota(jnp.int32, (1, LANES), 1)
    first_half = (lane % HEAD_DIM) < (HEAD_DIM // 2)
    sin_t = jnp.where(first_half, -sin_t, sin_t)

    def qk_norm_rope(t, gain_ref):
        w = t.shape[1]
        msq = _dot((t * t).astype(BF16), hsum_ref[0:w, 0:w])
        t = t * lax.rsqrt(msq + RMS_EPS) * gain_ref[...]
        reps = w // LANES
        cos_w = jnp.concatenate([cos_t] * reps, axis=1) if reps > 1 else cos_t
        sin_w = jnp.concatenate([sin_t] * reps, axis=1) if reps > 1 else sin_t
        fh = jnp.concatenate([first_half] * reps, axis=1) if reps > 1 else first_half
        half = HEAD_DIM // 2
        rot = jnp.where(fh, pltpu.roll(t, w - half, axis=1), pltpu.roll(t, half, axis=1))
        return t * cos_w + rot * sin_w

    q = _dot(h, w_in_ref[:, 0:ATTN_WIDTH])
    q = qk_norm_rope(q, qn_ref) * (HEAD_DIM ** -0.5)
    k = _dot(h, w_in_ref[:, ATTN_WIDTH:ATTN_WIDTH + KV_WIDTH])
    k = qk_norm_rope(k, kn_ref)
    v = _dot(h, w_in_ref[:, ATTN_WIDTH + KV_WIDTH:ATTN_WIDTH + 2 * KV_WIDTH])

    k_ext = jnp.concatenate([k_tail[...], k], axis=0)
    v_ext = jnp.concatenate([v_tail[...], v], axis=0)
    k_tail[...] = k[ts - ATTN_BLOCK:]
    v_tail[...] = v[ts - ATTN_BLOCK:]

    low = lane < HEAD_DIM
    k_sw = pltpu.roll(k_ext, HEAD_DIM, axis=1)
    v_sw = pltpu.roll(v_ext, HEAD_DIM, axis=1)
    zero = jnp.zeros_like(k_ext)
    k_lo = [jnp.where(low, k_ext, zero).astype(BF16), jnp.where(low, k_sw, zero).astype(BF16)]
    k_hi = [jnp.where(low, zero, k_sw).astype(BF16), jnp.where(low, zero, k_ext).astype(BF16)]
    v_lo = [jnp.where(low, v_ext, zero).astype(BF16), jnp.where(low, v_sw, zero).astype(BF16)]
    v_hi = [jnp.where(low, zero, v_sw).astype(BF16), jnp.where(low, zero, v_ext).astype(BF16)]

    qi = lax.broadcasted_iota(jnp.int32, (ATTN_BLOCK, 2 * ATTN_BLOCK), 0)
    kj = lax.broadcasted_iota(jnp.int32, (ATTN_BLOCK, 2 * ATTN_BLOCK), 1)
    band = (kj > qi) & (kj <= qi + ATTN_BLOCK)
    band_first = band & (kj >= jnp.where(ti > 0, 0, ATTN_BLOCK))

    q_bf = q.astype(BF16)
    slots_per_group = (ATTN_Q_HEADS // ATTN_KV_HEADS) // 2
    for jb in range(ts // ATTN_BLOCK):
        r0 = jb * ATTN_BLOCK
        mask = band_first if jb == 0 else band
        for slot in range(ATTN_Q_HEADS // 2):
            g = slot // slots_per_group
            qs = q_bf[r0:r0 + ATTN_BLOCK, slot * LANES:(slot + 1) * LANES]
            outs = []
            probs = []
            for half, kk in enumerate((k_lo[g], k_hi[g])):
                s = _dot_nt(qs, kk[r0:r0 + 2 * ATTN_BLOCK])
                s = jnp.where(mask, s, NEG_BIG)
                sk = sink_ref[2 * slot + half]
                m = jnp.maximum(jnp.max(s, axis=-1, keepdims=True), sk)
                p = jnp.exp(s - m)
                den = jnp.sum(p, axis=-1, keepdims=True) + jnp.exp(sk - m)
                probs.append((p / den).astype(BF16))
            o = (_dot(probs[0], v_lo[g][r0:r0 + 2 * ATTN_BLOCK])
                 + _dot(probs[1], v_hi[g][r0:r0 + 2 * ATTN_BLOCK]))
            attn_buf[r0:r0 + ATTN_BLOCK, slot * LANES:(slot + 1) * LANES] = o.astype(BF16)

    c0 = ATTN_WIDTH + 2 * KV_WIDTH
    cb = _dot(h, w_in_ref[:, c0:c0 + CONV_WIDTH])
    cc = _dot(h, w_in_ref[:, c0 + CONV_WIDTH:c0 + 2 * CONV_WIDTH])
    cx = _dot(h, w_in_ref[:, c0 + 2 * CONV_WIDTH:c0 + 3 * CONV_WIDTH])
    y = cc * cx
    tail = y_tail[...]
    conv = (conv_w_ref[2:3, :] * y + conv_w_ref[1:2, :] * _shift_rows(y, tail, 1)
            + conv_w_ref[0:1, :] * _shift_rows(y, tail, 2))
    y_tail[...] = y[ts - SUBLANES:]
    gated = (cb * conv).astype(BF16)

    out = _dot(attn_buf[...], w_out_ref[0:ATTN_WIDTH, :]) + _dot(gated, w_out_ref[ATTN_WIDTH:, :])
    o_ref[0] = x + out


def _hy_mixer(x, positions, norm, w_in, q_norm, k_norm, sinks, conv_w, w_out):
    b, s, d = x.shape
    ts = SEQ_TILE
    inv_freq = ROPE_THETA ** (-jnp.arange(0, HEAD_DIM, 2, dtype=F32) / HEAD_DIM)
    invf = jnp.broadcast_to(jnp.tile(inv_freq, LANES // (HEAD_DIM // 2))[:, None], (LANES, LANES))
    head_id = jnp.arange(ATTN_WIDTH) // HEAD_DIM
    hsum = jnp.where(head_id[:, None] == head_id[None, :], 1.0 / HEAD_DIM, 0.0).astype(BF16)
    qn = jnp.tile(q_norm, ATTN_Q_HEADS)[None, :]
    kn = jnp.tile(k_norm, ATTN_KV_HEADS)[None, :]
    const = lambda *shape: pl.BlockSpec(shape, lambda bi, ti: (0,) * len(shape),
                                        pipeline_mode=pl.Buffered(1))
    return pl.pallas_call(
        _hy_mixer_kernel,
        grid=(b, s // ts),
        in_specs=[
            pl.BlockSpec((1, ts, d), lambda bi, ti: (bi, ti, 0)),
            pl.BlockSpec((1, 1, ts), lambda bi, ti: (bi, 0, ti)),
            const(LANES, LANES),
            const(1, d),
            const(d, HY_IN_WIDTH),
            const(1, ATTN_WIDTH),
            const(1, KV_WIDTH),
            pl.BlockSpec(memory_space=pltpu.SMEM),
            const(ATTN_WIDTH, ATTN_WIDTH),
            const(3, CONV_WIDTH),
            const(d, d),
        ],
        out_specs=pl.BlockSpec((1, ts, d), lambda bi, ti: (bi, ti, 0)),
        out_shape=jax.ShapeDtypeStruct(x.shape, x.dtype),
        scratch_shapes=[
            pltpu.VMEM((ATTN_BLOCK, KV_WIDTH), F32),
            pltpu.VMEM((ATTN_BLOCK, KV_WIDTH), F32),
            pltpu.VMEM((SUBLANES, CONV_WIDTH), F32),
            pltpu.VMEM((ts, ATTN_WIDTH), BF16),
        ],
        compiler_params=pltpu.CompilerParams(
            dimension_semantics=("arbitrary", "arbitrary"), vmem_limit_bytes=VMEM_LIMIT),
        name="hy_mixer",
    )(x, positions.reshape(b, 1, s), invf, norm[None, :], w_in.astype(BF16), qn, kn, sinks, hsum,
      conv_w, w_out.astype(BF16))


def _gla_mixer_kernel(x_ref, norm_ref, w_qkv_ref, w_glr_ref, w_og_ref, w_gup_ref, gbias_ref, tril_ref,
                      onorm_ref, w_out_ref, o_ref, state, o_buf):
    ti = pl.program_id(1)
    ts = x_ref.shape[1]
    c = GLA_CHUNK

    @pl.when(ti == 0)
    def _():
        state[...] = jnp.zeros_like(state)

    x = x_ref[0]
    h = _rms_norm(x, norm_ref[...]).astype(BF16)

    g_lr = _dot(h, w_glr_ref[...])
    z = _dot(g_lr.astype(BF16), w_gup_ref[...]) + gbias_ref[...]
    log_a = (jnp.minimum(z, 0.0) - jnp.log1p(jnp.exp(-jnp.abs(z)))) / GLA_GATE_NORMALIZER
    la_hi = log_a.astype(BF16)
    la_lo = (log_a - la_hi.astype(F32)).astype(BF16)

    q_all = _dot(h, w_qkv_ref[:, 0:GLA_KEY_DIM]) * (GLA_DK ** -0.5)
    k_all = _dot(h, w_qkv_ref[:, GLA_KEY_DIM:2 * GLA_KEY_DIM])
    v_all = _dot(h, w_qkv_ref[:, 2 * GLA_KEY_DIM:]).astype(BF16)

    ri = lax.broadcasted_iota(jnp.int32, (c, c), 0)
    cj = lax.broadcasted_iota(jnp.int32, (c, c), 1)
    causal = cj <= ri
    nsub = c // GLA_SUB
    tril = tril_ref[...]

    for ci in range(ts // c):
        r0 = ci * c
        gc = (_dot(tril, la_hi[r0:r0 + c]) + _dot(tril, la_lo[r0:r0 + c]))
        for hd in range(GLA_HEADS):
            ks = slice(hd * GLA_DK, (hd + 1) * GLA_DK)
            g = gc[:, ks]
            qh = q_all[r0:r0 + c, ks]
            kh = k_all[r0:r0 + c, ks]
            vh = v_all[r0:r0 + c, hd * GLA_DV:(hd + 1) * GLA_DV]
            g_end = g[c - 1:c, :]
            st = state[hd]
            o = _dot_nt((qh * jnp.exp(g)).astype(BF16), st.astype(BF16))
            mid = GLA_SUB // 2
            g_refs = [g[sb * GLA_SUB + mid:sb * GLA_SUB + mid + 1, :] for sb in range(nsub)]
            g_own = jnp.concatenate([jnp.broadcast_to(r, (GLA_SUB, GLA_DK)) for r in g_refs], axis=0)
            qt = (qh * jnp.exp(g - g_own)).astype(BF16)
            zq = jnp.zeros((GLA_SUB, GLA_DK), BF16)
            q_cols, k_cols = [], []
            for sb in range(nsub):
                s0 = sb * GLA_SUB
                n = s0 + GLA_SUB
                q_cols.append(jnp.concatenate([zq] * sb + [qt[s0:n]] + [zq] * (nsub - 1 - sb), axis=0))
                kt = (kh[0:n] * jnp.exp(g_refs[sb] - g[0:n])).astype(BF16)
                k_cols.append(jnp.concatenate([kt] + [zq] * (nsub - 1 - sb), axis=0))
            sc = _dot_nt(jnp.concatenate(q_cols, axis=1), jnp.concatenate(k_cols, axis=1))
            scores = jnp.where(causal, sc, 0.0).astype(BF16)
            o = o + _dot(scores, vh)
            k_dec = (kh * jnp.exp(g_end - g)).astype(BF16)
            state[hd] = jnp.exp(g_end) * st + _dot_tn(vh, k_dec)
            o = o * lax.rsqrt(jnp.mean(o * o, axis=-1, keepdims=True) + RMS_EPS) * onorm_ref[...]
            o_buf[r0:r0 + c, hd * GLA_DV:(hd + 1) * GLA_DV] = o

    og = _dot(h, w_og_ref[...])
    gated = (o_buf[...] * (og * jax.nn.sigmoid(og))).astype(BF16)
    o_ref[0] = x + _dot(gated, w_out_ref[...])


def _gla_mixer(x, norm, w_in, w_gate_up, gate_bias, o_norm, w_out):
    b, s, d = x.shape
    ts = SEQ_TILE
    c_qkv = 2 * GLA_KEY_DIM + GLA_VAL_DIM
    w_qkv = w_in[:, :c_qkv].astype(BF16)
    w_glr = jnp.pad(w_in[:, c_qkv:c_qkv + GLA_GATE_RANK], ((0, 0), (0, LANES - GLA_GATE_RANK))).astype(BF16)
    w_og = w_in[:, c_qkv + GLA_GATE_RANK:].astype(BF16)
    w_gup = jnp.pad(w_gate_up, ((0, LANES - GLA_GATE_RANK), (0, 0))).astype(BF16)
    tril = jnp.tril(jnp.ones((GLA_CHUNK, GLA_CHUNK), F32)).astype(BF16)
    const = lambda *shape: pl.BlockSpec(shape, lambda bi, ti: (0,) * len(shape),
                                        pipeline_mode=pl.Buffered(1))
    return pl.pallas_call(
        _gla_mixer_kernel,
        grid=(b, s // ts),
        in_specs=[
            pl.BlockSpec((1, ts, d), lambda bi, ti: (bi, ti, 0)),
            const(1, d),
            const(d, c_qkv),
            const(d, LANES),
            const(d, GLA_VAL_DIM),
            const(LANES, GLA_KEY_DIM),
            const(1, GLA_KEY_DIM),
            const(GLA_CHUNK, GLA_CHUNK),
            const(1, GLA_DV),
            const(GLA_VAL_DIM, d),
        ],
        out_specs=pl.BlockSpec((1, ts, d), lambda bi, ti: (bi, ti, 0)),
        out_shape=jax.ShapeDtypeStruct(x.shape, x.dtype),
        scratch_shapes=[
            pltpu.VMEM((GLA_HEADS, GLA_DV, GLA_DK), F32),
            pltpu.VMEM((ts, GLA_VAL_DIM), F32),
        ],
        compiler_params=pltpu.CompilerParams(
            dimension_semantics=("arbitrary", "arbitrary"), vmem_limit_bytes=VMEM_LIMIT),
        name="gla_mixer",
    )(x, norm[None, :], w_qkv, w_glr, w_og, w_gup, gate_bias[None, :], tril, o_norm[None, :],
      w_out.astype(BF16))


def _ffn_ple_kernel(x_ref, p_ref, fnorm_ref, w_up_ref, conv_w_ref, conv_b_ref, w_down_ref, pnorm_ref,
                    w_gate_ref, w_proj_ref, o_ref, u_tail):
    ti = pl.program_id(1)
    ts = x_ref.shape[1]

    @pl.when(ti == 0)
    def _():
        u_tail[...] = jnp.zeros_like(u_tail)

    x = x_ref[0]
    h = _rms_norm(x, fnorm_ref[...]).astype(BF16)

    def conv_cols(c0):
        cols = slice(c0, c0 + FF_CHUNK)
        u = _dot(h, w_up_ref[:, cols])
        tail = u_tail[:, cols]
        u_tail[:, cols] = u[ts - SUBLANES:]
        return (conv_w_ref[2:3, cols] * u + conv_w_ref[1:2, cols] * _shift_rows(u, tail, 1)
                + conv_w_ref[0:1, cols] * _shift_rows(u, tail, 2) + conv_b_ref[:, cols])

    acc = jnp.zeros((ts, D_MODEL), F32)
    for ci in range(D_FF // FF_CHUNK):
        gate = conv_cols(ci * FF_CHUNK)
        up = conv_cols(D_FF + ci * FF_CHUNK)
        gelu = 0.5 * gate * (1.0 + lax.erf(gate * (2.0 ** -0.5)))
        act = (gelu * up).astype(BF16)
        acc = acc + _dot(act, w_down_ref[ci * FF_CHUNK:(ci + 1) * FF_CHUNK, :])
    x1 = x + acc

    hp = _rms_norm(x1, pnorm_ref[...]).astype(BF16)
    gate = jax.nn.sigmoid(_dot(hp, w_gate_ref[...]))
    emb = _dot(p_ref[0].astype(BF16), w_proj_ref[...])
    o_ref[0] = x1 + gate * emb


def _ffn_ple(x, p_i, fnorm, w_up, conv_w, conv_b, w_down, pnorm, w_gate, w_proj):
    b, s, d = x.shape
    ts = SEQ_TILE
    const = lambda *shape: pl.BlockSpec(shape, lambda bi, ti: (0,) * len(shape),
                                        pipeline_mode=pl.Buffered(1))
    return pl.pallas_call(
        _ffn_ple_kernel,
        grid=(b, s // ts),
        in_specs=[
            pl.BlockSpec((1, ts, d), lambda bi, ti: (bi, ti, 0)),
            pl.BlockSpec((1, ts, PLE_DIM), lambda bi, ti: (bi, ti, 0)),
            const(1, d),
            const(d, 2 * D_FF),
            const(3, 2 * D_FF),
            const(1, 2 * D_FF),
            const(D_FF, d),
            const(1, d),
            const(d, d),
            const(PLE_DIM, d),
        ],
        out_specs=pl.BlockSpec((1, ts, d), lambda bi, ti: (bi, ti, 0)),
        out_shape=jax.ShapeDtypeStruct(x.shape, x.dtype),
        scratch_shapes=[pltpu.VMEM((SUBLANES, 2 * D_FF), F32)],
        compiler_params=pltpu.CompilerParams(
            dimension_semantics=("arbitrary", "arbitrary"), vmem_limit_bytes=VMEM_LIMIT),
        name="ffn_ple",
    )(x, p_i, fnorm[None, :], w_up.astype(BF16), conv_w, conv_b[None, :], w_down.astype(BF16),
      pnorm[None, :], w_gate.astype(BF16), w_proj.astype(BF16))


def kernel(x, p, positions, mix_norm, ffn_norm, ffn_w_up, ffn_conv_w, ffn_conv_b, ffn_w_down, ple_norm,
           ple_w_gate, ple_w_proj, hy_w_in, hy_q_norm, hy_k_norm, hy_sinks, hy_conv_w, hy_w_out, gla_w_in,
           gla_w_gate_up, gla_gate_bias, gla_o_norm, gla_w_out):
    depth = mix_norm.shape[0]
    for i in range(depth):
        j = i // 2
        if i % 2 == 0:
            x = _hy_mixer(x, positions, mix_norm[i], hy_w_in[j], hy_q_norm[j], hy_k_norm[j], hy_sinks[j],
                          hy_conv_w[j], hy_w_out[j])
        else:
            x = _gla_mixer(x, mix_norm[i], gla_w_in[j], gla_w_gate_up[j], gla_gate_bias[j], gla_o_norm[j],
                           gla_w_out[j])
        x = _ffn_ple(x, p[i], ffn_norm[i], ffn_w_up[i], ffn_conv_w[i], ffn_conv_b[i], ffn_w_down[i],
                     ple_norm[i], ple_w_gate[i], ple_w_proj[i])
    return x
```

```python
import functools

import jax
import jax.numpy as jnp
from jax import lax
from jax.experimental import pallas as pl
from jax.experimental.pallas import tpu as pltpu

D_MODEL = 1024
HEAD_DIM = 64
ATTN_Q_HEADS = 8
ATTN_KV_HEADS = 2
ATTN_BLOCK = 128
ROPE_THETA = 10000.0
ATTN_WIDTH = ATTN_Q_HEADS * HEAD_DIM
KV_WIDTH = ATTN_KV_HEADS * HEAD_DIM
CONV_WIDTH = D_MODEL // 2
HY_IN_WIDTH = ATTN_WIDTH + 2 * KV_WIDTH + 3 * CONV_WIDTH
GLA_HEADS = 4
GLA_KEY_DIM = D_MODEL // 2
GLA_VAL_DIM = D_MODEL
GLA_DK = GLA_KEY_DIM // GLA_HEADS
GLA_DV = GLA_VAL_DIM // GLA_HEADS
GLA_GATE_RANK = 16
GLA_GATE_NORMALIZER = 16.0
GLA_CHUNK = 64
GLA_SUB = 16
D_FF = 2816
PLE_DIM = 256
RMS_EPS = 1e-6

LANES = 128
SUBLANES = 8
SEQ_TILE = 512
FFN_SEQ_TILE = 256
FF_CHUNK = 256
VMEM_LIMIT = 56 * 1024 * 1024
NEG_BIG = -1e30

F32 = jnp.float32
BF16 = jnp.bfloat16


def _dot(a, b):
    return jnp.dot(a, b, preferred_element_type=F32)


def _dot_nt(a, b):
    return lax.dot_general(a, b, (((1,), (1,)), ((), ())), preferred_element_type=F32)


def _dot_tn(a, b):
    return lax.dot_general(a, b, (((0,), (0,)), ((), ())), preferred_element_type=F32)


def _rms_norm(x, gain):
    return x * lax.rsqrt(jnp.mean(x * x, axis=-1, keepdims=True) + RMS_EPS) * gain


def _shift_rows(u, tail, k):
    n = u.shape[0]
    ext = jnp.concatenate([tail, u], axis=0)
    return ext[SUBLANES - k:SUBLANES - k + n]


def _hy_mixer_kernel(x_ref, pos_ref, invf_ref, norm_ref, w_in_ref, qn_ref, kn_ref, sink_ref, hsum_ref,
                     conv_w_ref, w_out_ref, o_ref, k_tail, v_tail, y_tail, attn_buf, s_buf, p_buf, inv_buf):
    ti = pl.program_id(1)
    ts = x_ref.shape[1]

    @pl.when(ti == 0)
    def _():
        k_tail[...] = jnp.zeros_like(k_tail)
        v_tail[...] = jnp.zeros_like(v_tail)
        y_tail[...] = jnp.zeros_like(y_tail)

    x = x_ref[0]
    h = _rms_norm(x, norm_ref[...]).astype(BF16)

    pos = pos_ref[0].astype(F32)
    n_freq = HEAD_DIM // 2
    ang = invf_ref[0:n_freq, 0:1] * pos
    cos_t = jnp.concatenate([jnp.cos(ang)] * (LANES // n_freq), axis=0).T
    sin_t = jnp.concatenate([jnp.sin(ang)] * (LANES // n_freq), axis=0).T
    lane = lax.broadcasted_iota(jnp.int32, (1, LANES), 1)
    first_half = (lane % HEAD_DIM) < (HEAD_DIM // 2)
    sin_t = jnp.where(first_half, -sin_t, sin_t)

    def qk_norm_rope(t, gain_ref):
        w = t.shape[1]
        msq = _dot((t * t).astype(BF16), hsum_ref[0:w, 0:w])
        t = t * lax.rsqrt(msq + RMS_EPS) * gain_ref[...]
        reps = w // LANES
        cos_w = jnp.concatenate([cos_t] * reps, axis=1) if reps > 1 else cos_t
        sin_w = jnp.concatenate([sin_t] * reps, axis=1) if reps > 1 else sin_t
        fh = jnp.concatenate([first_half] * reps, axis=1) if reps > 1 else first_half
        half = HEAD_DIM // 2
        rot = jnp.where(fh, pltpu.roll(t, w - half, axis=1), pltpu.roll(t, half, axis=1))
        return t * cos_w + rot * sin_w

    q = _dot(h, w_in_ref[:, 0:ATTN_WIDTH])
    q = qk_norm_rope(q, qn_ref) * (HEAD_DIM ** -0.5)
    k = _dot(h, w_in_ref[:, ATTN_WIDTH:ATTN_WIDTH + KV_WIDTH])
    k = qk_norm_rope(k, kn_ref)
    v = _dot(h, w_in_ref[:, ATTN_WIDTH + KV_WIDTH:ATTN_WIDTH + 2 * KV_WIDTH])

    k_ext = jnp.concatenate([k_tail[...], k], axis=0)
    v_ext = jnp.concatenate([v_tail[...], v], axis=0)
    k_tail[...] = k[ts - ATTN_BLOCK:]
    v_tail[...] = v[ts - ATTN_BLOCK:]

    low = lane < HEAD_DIM
    k_sw = pltpu.roll(k_ext, HEAD_DIM, axis=1)
    v_sw = pltpu.roll(v_ext, HEAD_DIM, axis=1)
    zero = jnp.zeros_like(k_ext)
    k_lo = [jnp.where(low, k_ext, zero).astype(BF16), jnp.where(low, k_sw, zero).astype(BF16)]
    k_hi = [jnp.where(low, zero, k_sw).astype(BF16), jnp.where(low, zero, k_ext).astype(BF16)]
    v_lo = [jnp.where(low, v_ext, zero).astype(BF16), jnp.where(low, v_sw, zero).astype(BF16)]
    v_hi = [jnp.where(low, zero, v_sw).astype(BF16), jnp.where(low, zero, v_ext).astype(BF16)]

    n_blocks = ts // ATTN_BLOCK
    pair = 2 * ATTN_BLOCK
    qi = lax.broadcasted_iota(jnp.int32, (pair, pair), 0) % ATTN_BLOCK
    kj = lax.broadcasted_iota(jnp.int32, (pair, pair), 1)
    band = (kj > qi) & (kj <= qi + ATTN_BLOCK)
    band_first = band & (kj >= jnp.where(ti > 0, 0, ATTN_BLOCK))
    upper_rows = lax.broadcasted_iota(jnp.int32, (pair, 1), 0) >= ATTN_BLOCK

    q_bf = q.astype(BF16)
    for jb in range(n_blocks):
        r0 = jb * ATTN_BLOCK
        for g in range(ATTN_KV_HEADS):
            qs = jnp.concatenate([q_bf[r0:r0 + ATTN_BLOCK, (2 * g + i) * LANES:(2 * g + i + 1) * LANES]
                                  for i in range(2)], axis=0)
            kk = jnp.concatenate([k_lo[g][r0:r0 + pair], k_hi[g][r0:r0 + pair]], axis=0)
            s_buf[jb, g] = _dot_nt(qs, kk)

    c0 = ATTN_WIDTH + 2 * KV_WIDTH
    cb = _dot(h, w_in_ref[:, c0:c0 + CONV_WIDTH])
    cc = _dot(h, w_in_ref[:, c0 + CONV_WIDTH:c0 + 2 * CONV_WIDTH])
    cx = _dot(h, w_in_ref[:, c0 + 2 * CONV_WIDTH:c0 + 3 * CONV_WIDTH])

    for jb in range(n_blocks):
        mask = band_first if jb == 0 else band
        for g in range(ATTN_KV_HEADS):
            for half in range(2):
                s = jnp.where(mask, s_buf[jb, g, :, half * pair:(half + 1) * pair], NEG_BIG)
                sk = jnp.where(upper_rows, sink_ref[4 * g + 2 + half], sink_ref[4 * g + half])
                m = jnp.maximum(jnp.max(s, axis=-1, keepdims=True), sk)
                p = jnp.exp(s - m)
                inv_buf[jb, g, half] = 1.0 / (jnp.sum(p, axis=-1, keepdims=True) + jnp.exp(sk - m))
                p_buf[jb, g, :, half * pair:(half + 1) * pair] = p.astype(BF16)

    for jb in range(n_blocks):
        r0 = jb * ATTN_BLOCK
        for g in range(ATTN_KV_HEADS):
            vv = jnp.concatenate([v_lo[g][r0:r0 + pair], v_hi[g][r0:r0 + pair]], axis=0)
            o = _dot(p_buf[jb, g], vv)
            o = o * jnp.where(low, inv_buf[jb, g, 0], inv_buf[jb, g, 1])
            for i in range(2):
                attn_buf[r0:r0 + ATTN_BLOCK, (2 * g + i) * LANES:(2 * g + i + 1) * LANES] = (
                    o[i * ATTN_BLOCK:(i + 1) * ATTN_BLOCK].astype(BF16))

    y = cc * cx
    tail = y_tail[...]
    conv = (conv_w_ref[2:3, :] * y + conv_w_ref[1:2, :] * _shift_rows(y, tail, 1)
            + conv_w_ref[0:1, :] * _shift_rows(y, tail, 2))
    y_tail[...] = y[ts - SUBLANES:]
    gated = (cb * conv).astype(BF16)

    out = _dot(attn_buf[...], w_out_ref[0:ATTN_WIDTH, :]) + _dot(gated, w_out_ref[ATTN_WIDTH:, :])
    o_ref[0] = x + out


def _hy_mixer(x, positions, norm, w_in, q_norm, k_norm, sinks, conv_w, w_out):
    b, s, d = x.shape
    ts = SEQ_TILE
    inv_freq = ROPE_THETA ** (-jnp.arange(0, HEAD_DIM, 2, dtype=F32) / HEAD_DIM)
    invf = jnp.broadcast_to(jnp.tile(inv_freq, LANES // (HEAD_DIM // 2))[:, None], (LANES, LANES))
    head_id = jnp.arange(ATTN_WIDTH) // HEAD_DIM
    hsum = jnp.where(head_id[:, None] == head_id[None, :], 1.0 / HEAD_DIM, 0.0).astype(BF16)
    qn = jnp.tile(q_norm, ATTN_Q_HEADS)[None, :]
    kn = jnp.tile(k_norm, ATTN_KV_HEADS)[None, :]
    const = lambda *shape: pl.BlockSpec(shape, lambda bi, ti: (0,) * len(shape),
                                        pipeline_mode=pl.Buffered(1))
    return pl.pallas_call(
        _hy_mixer_kernel,
        grid=(b, s // ts),
        in_specs=[
            pl.BlockSpec((1, ts, d), lambda bi, ti: (bi, ti, 0)),
            pl.BlockSpec((1, 1, ts), lambda bi, ti: (bi, 0, ti)),
            const(LANES, LANES),
            const(1, d),
            const(d, HY_IN_WIDTH),
            const(1, ATTN_WIDTH),
            const(1, KV_WIDTH),
            pl.BlockSpec(memory_space=pltpu.SMEM),
            const(ATTN_WIDTH, ATTN_WIDTH),
            const(3, CONV_WIDTH),
            const(d, d),
        ],
        out_specs=pl.BlockSpec((1, ts, d), lambda bi, ti: (bi, ti, 0)),
        out_shape=jax.ShapeDtypeStruct(x.shape, x.dtype),
        scratch_shapes=[
            pltpu.VMEM((ATTN_BLOCK, KV_WIDTH), F32),
            pltpu.VMEM((ATTN_BLOCK, KV_WIDTH), F32),
            pltpu.VMEM((SUBLANES, CONV_WIDTH), F32),
            pltpu.VMEM((ts, ATTN_WIDTH), BF16),
            pltpu.VMEM((ts // ATTN_BLOCK, ATTN_KV_HEADS, 2 * ATTN_BLOCK, 4 * ATTN_BLOCK), F32),
            pltpu.VMEM((ts // ATTN_BLOCK, ATTN_KV_HEADS, 2 * ATTN_BLOCK, 4 * ATTN_BLOCK), BF16),
            pltpu.VMEM((ts // ATTN_BLOCK, ATTN_KV_HEADS, 2, 2 * ATTN_BLOCK, 1), F32),
        ],
        compiler_params=pltpu.CompilerParams(
            dimension_semantics=("arbitrary", "arbitrary"), vmem_limit_bytes=VMEM_LIMIT),
        name="hy_mixer",
    )(x, positions.reshape(b, 1, s), invf, norm[None, :], w_in.astype(BF16), qn, kn, sinks, hsum,
      conv_w, w_out.astype(BF16))


def _gla_mixer_kernel(x_ref, norm_ref, w_qkv_ref, w_glr_ref, w_og_ref, w_gup_ref, gbias_ref, tril_ref,
                      onorm_ref, w_out_ref, o_ref, state, o_buf):
    ti = pl.program_id(1)
    ts = x_ref.shape[1]
    c = GLA_CHUNK

    @pl.when(ti == 0)
    def _():
        state[...] = jnp.zeros_like(state)

    x = x_ref[0]
    h = _rms_norm(x, norm_ref[...]).astype(BF16)

    g_lr = _dot(h, w_glr_ref[...])
    z = _dot(g_lr.astype(BF16), w_gup_ref[...]) + gbias_ref[...]
    log_a = (jnp.minimum(z, 0.0) - jnp.log1p(jnp.exp(-jnp.abs(z)))) / GLA_GATE_NORMALIZER
    la_hi = log_a.astype(BF16)
    la_lo = (log_a - la_hi.astype(F32)).astype(BF16)

    q_all = _dot(h, w_qkv_ref[:, 0:GLA_KEY_DIM]) * (GLA_DK ** -0.5)
    k_all = _dot(h, w_qkv_ref[:, GLA_KEY_DIM:2 * GLA_KEY_DIM])
    v_all = _dot(h, w_qkv_ref[:, 2 * GLA_KEY_DIM:]).astype(BF16)

    ri = lax.broadcasted_iota(jnp.int32, (c, c), 0)
    cj = lax.broadcasted_iota(jnp.int32, (c, c), 1)
    causal = cj <= ri
    nsub = c // GLA_SUB
    tril = tril_ref[...]

    for ci in range(ts // c):
        r0 = ci * c
        gc = (_dot(tril, la_hi[r0:r0 + c]) + _dot(tril, la_lo[r0:r0 + c]))
        for hd in range(GLA_HEADS):
            ks = slice(hd * GLA_DK, (hd + 1) * GLA_DK)
            g = gc[:, ks]
            qh = q_all[r0:r0 + c, ks]
            kh = k_all[r0:r0 + c, ks]
            vh = v_all[r0:r0 + c, hd * GLA_DV:(hd + 1) * GLA_DV]
            g_end = g[c - 1:c, :]
            st = state[hd]
            o = _dot_nt((qh * jnp.exp(g)).astype(BF16), st.astype(BF16))
            mid = GLA_SUB // 2
            g_refs = [g[sb * GLA_SUB + mid:sb * GLA_SUB + mid + 1, :] for sb in range(nsub)]
            g_own = jnp.concatenate([jnp.broadcast_to(r, (GLA_SUB, GLA_DK)) for r in g_refs], axis=0)
            qt = (qh * jnp.exp(g - g_own)).astype(BF16)
            zq = jnp.zeros((GLA_SUB, GLA_DK), BF16)
            q_cols, k_cols = [], []
            for sb in range(nsub):
                s0 = sb * GLA_SUB
                n = s0 + GLA_SUB
                q_cols.append(jnp.concatenate([zq] * sb + [qt[s0:n]] + [zq] * (nsub - 1 - sb), axis=0))
                kt = (kh[0:n] * jnp.exp(g_refs[sb] - g[0:n])).astype(BF16)
                k_cols.append(jnp.concatenate([kt] + [zq] * (nsub - 1 - sb), axis=0))
            sc = _dot_nt(jnp.concatenate(q_cols, axis=1), jnp.concatenate(k_cols, axis=1))
            scores = jnp.where(causal, sc, 0.0).astype(BF16)
            o = o + _dot(scores, vh)
            k_dec = (kh * jnp.exp(g_end - g)).astype(BF16)
            state[hd] = jnp.exp(g_end) * st + _dot_tn(vh, k_dec)
            o = o * lax.rsqrt(jnp.mean(o * o, axis=-1, keepdims=True) + RMS_EPS) * onorm_ref[...]
            o_buf[r0:r0 + c, hd * GLA_DV:(hd + 1) * GLA_DV] = o

    og = _dot(h, w_og_ref[...])
    gated = (o_buf[...] * (og * jax.nn.sigmoid(og))).astype(BF16)
    o_ref[0] = x + _dot(gated, w_out_ref[...])


def _gla_mixer(x, norm, w_in, w_gate_up, gate_bias, o_norm, w_out):
    b, s, d = x.shape
    ts = SEQ_TILE
    c_qkv = 2 * GLA_KEY_DIM + GLA_VAL_DIM
    w_qkv = w_in[:, :c_qkv].astype(BF16)
    w_glr = jnp.pad(w_in[:, c_qkv:c_qkv + GLA_GATE_RANK], ((0, 0), (0, LANES - GLA_GATE_RANK))).astype(BF16)
    w_og = w_in[:, c_qkv + GLA_GATE_RANK:].astype(BF16)
    w_gup = jnp.pad(w_gate_up, ((0, LANES - GLA_GATE_RANK), (0, 0))).astype(BF16)
    tril = jnp.tril(jnp.ones((GLA_CHUNK, GLA_CHUNK), F32)).astype(BF16)
    const = lambda *shape: pl.BlockSpec(shape, lambda bi, ti: (0,) * len(shape),
                                        pipeline_mode=pl.Buffered(1))
    return pl.pallas_call(
        _gla_mixer_kernel,
        grid=(b, s // ts),
        in_specs=[
            pl.BlockSpec((1, ts, d), lambda bi, ti: (bi, ti, 0)),
            const(1, d),
            const(d, c_qkv),
            const(d, LANES),
            const(d, GLA_VAL_DIM),
            const(LANES, GLA_KEY_DIM),
            const(1, GLA_KEY_DIM),
            const(GLA_CHUNK, GLA_CHUNK),
            const(1, GLA_DV),
            const(GLA_VAL_DIM, d),
        ],
        out_specs=pl.BlockSpec((1, ts, d), lambda bi, ti: (bi, ti, 0)),
        out_shape=jax.ShapeDtypeStruct(x.shape, x.dtype),
        scratch_shapes=[
            pltpu.VMEM((GLA_HEADS, GLA_DV, GLA_DK), F32),
            pltpu.VMEM((ts, GLA_VAL_DIM), F32),
        ],
        compiler_params=pltpu.CompilerParams(
            dimension_semantics=("arbitrary", "arbitrary"), vmem_limit_bytes=VMEM_LIMIT),
        name="gla_mixer",
    )(x, norm[None, :], w_qkv, w_glr, w_og, w_gup, gate_bias[None, :], tril, o_norm[None, :],
      w_out.astype(BF16))


def _ffn_ple_kernel(x_ref, p_ref, fnorm_ref, w_up_ref, conv_w_ref, conv_b_ref, w_down_ref, pnorm_ref,
                    w_gate_ref, w_proj_ref, o_ref, s0_tail, s1_tail, act_buf):
    ti = pl.program_id(1)
    ts = x_ref.shape[1]

    @pl.when(ti == 0)
    def _():
        s0_tail[...] = jnp.zeros_like(s0_tail)
        s1_tail[...] = jnp.zeros_like(s1_tail)

    x = x_ref[0]
    h = _rms_norm(x, fnorm_ref[...]).astype(BF16)

    def up_dots(ci):
        return (_dot(h, w_up_ref[:, ci * FF_CHUNK:(ci + 1) * FF_CHUNK]),
                _dot(h, w_up_ref[:, D_FF + ci * FF_CHUNK:D_FF + (ci + 1) * FF_CHUNK]))

    def conv_cols(u, c0):
        cols = slice(c0, c0 + FF_CHUNK)
        s0 = conv_w_ref[0:1, cols] * u
        s1 = conv_w_ref[1:2, cols] * u + _shift_rows(s0, s0_tail[:, cols], 1)
        out = conv_w_ref[2:3, cols] * u + _shift_rows(s1, s1_tail[:, cols], 1) + conv_b_ref[:, cols]
        s0_tail[:, cols] = s0[ts - SUBLANES:]
        s1_tail[:, cols] = s1[ts - SUBLANES:]
        return out

    n_chunks = D_FF // FF_CHUNK
    nxt = up_dots(0)
    for ci in range(n_chunks):
        u_gate, u_up = nxt
        if ci + 1 < n_chunks:
            nxt = up_dots(ci + 1)
        gate = conv_cols(u_gate, ci * FF_CHUNK)
        up = conv_cols(u_up, D_FF + ci * FF_CHUNK)
        gelu = 0.5 * gate * (1.0 + lax.erf(gate * (2.0 ** -0.5)))
        act_buf[:, ci * FF_CHUNK:(ci + 1) * FF_CHUNK] = (gelu * up).astype(BF16)
    x1 = x + _dot(act_buf[...], w_down_ref[...])

    hp = _rms_norm(x1, pnorm_ref[...]).astype(BF16)
    gate = jax.nn.sigmoid(_dot(hp, w_gate_ref[...]))
    emb = _dot(p_ref[0].astype(BF16), w_proj_ref[...])
    o_ref[0] = x1 + gate * emb


def _ffn_ple(x, p, layer, fnorm, w_up, conv_w, conv_b, w_down, pnorm, w_gate, w_proj):
    b, s, d = x.shape
    ts = FFN_SEQ_TILE
    const = lambda *shape: pl.BlockSpec(shape, lambda bi, ti: (0,) * len(shape),
                                        pipeline_mode=pl.Buffered(1))
    return pl.pallas_call(
        _ffn_ple_kernel,
        grid=(b, s // ts),
        in_specs=[
            pl.BlockSpec((1, ts, d), lambda bi, ti: (bi, ti, 0)),
            pl.BlockSpec((None, 1, ts, PLE_DIM), lambda bi, ti: (layer, bi, ti, 0)),
            const(1, d),
            const(d, 2 * D_FF),
            const(3, 2 * D_FF),
            const(1, 2 * D_FF),
            const(D_FF, d),
            const(1, d),
            const(d, d),
            const(PLE_DIM, d),
        ],
        out_specs=pl.BlockSpec((1, ts, d), lambda bi, ti: (bi, ti, 0)),
        out_shape=jax.ShapeDtypeStruct(x.shape, x.dtype),
        scratch_shapes=[pltpu.VMEM((SUBLANES, 2 * D_FF), F32), pltpu.VMEM((SUBLANES, 2 * D_FF), F32),
                        pltpu.VMEM((ts, D_FF), BF16)],
        compiler_params=pltpu.CompilerParams(
            dimension_semantics=("arbitrary", "arbitrary"), vmem_limit_bytes=VMEM_LIMIT),
        name="ffn_ple",
    )(x, p, fnorm[None, :], w_up.astype(BF16), conv_w, conv_b[None, :], w_down.astype(BF16),
      pnorm[None, :], w_gate.astype(BF16), w_proj.astype(BF16))


def kernel(x, p, positions, mix_norm, ffn_norm, ffn_w_up, ffn_conv_w, ffn_conv_b, ffn_w_down, ple_norm,
           ple_w_gate, ple_w_proj, hy_w_in, hy_q_norm, hy_k_norm, hy_sinks, hy_conv_w, hy_w_out, gla_w_in,
           gla_w_gate_up, gla_gate_bias, gla_o_norm, gla_w_out):
    depth = mix_norm.shape[0]
    for i in range(depth):
        j = i // 2
        if i % 2 == 0:
            x = _hy_mixer(x, positions, mix_norm[i], hy_w_in[j], hy_q_norm[j], hy_k_norm[j], hy_sinks[j],
                          hy_conv_w[j], hy_w_out[j])
        else:
            x = _gla_mixer(x, mix_norm[i], gla_w_in[j], gla_w_gate_up[j], gla_gate_bias[j], gla_o_norm[j],
                           gla_w_out[j])
        x = _ffn_ple(x, p, i, ffn_norm[i], ffn_w_up[i], ffn_conv_w[i], ffn_conv_b[i], ffn_w_down[i],
                     ple_norm[i], ple_w_gate[i], ple_w_proj[i])
    return x
```

```python
import functools

import jax
import jax.numpy as jnp
from jax import lax
from jax.experimental import pallas as pl
from jax.experimental.pallas import tpu as pltpu

D_MODEL = 1024
HEAD_DIM = 64
ATTN_Q_HEADS = 8
ATTN_KV_HEADS = 2
ATTN_BLOCK = 128
ROPE_THETA = 10000.0
ATTN_WIDTH = ATTN_Q_HEADS * HEAD_DIM
KV_WIDTH = ATTN_KV_HEADS * HEAD_DIM
CONV_WIDTH = D_MODEL // 2
HY_IN_WIDTH = ATTN_WIDTH + 2 * KV_WIDTH + 3 * CONV_WIDTH
GLA_HEADS = 4
GLA_KEY_DIM = D_MODEL // 2
GLA_VAL_DIM = D_MODEL
GLA_DK = GLA_KEY_DIM // GLA_HEADS
GLA_DV = GLA_VAL_DIM // GLA_HEADS
GLA_GATE_RANK = 16
GLA_GATE_NORMALIZER = 16.0
GLA_CHUNK = 64
GLA_TRIL_ROWS = 256
GLA_SUB = 16
D_FF = 2816
PLE_DIM = 256
RMS_EPS = 1e-6

LANES = 128
SUBLANES = 8
SEQ_TILE = 512
FFN_SEQ_TILE = 512
FFN_ROW_BLOCK = 256
FF_CHUNK = 256
VMEM_LIMIT = 56 * 1024 * 1024
NEG_BIG = -1e30

F32 = jnp.float32
BF16 = jnp.bfloat16


def _dot(a, b):
    return jnp.dot(a, b, preferred_element_type=F32)


def _dot_nt(a, b):
    return lax.dot_general(a, b, (((1,), (1,)), ((), ())), preferred_element_type=F32)


def _dot_tn(a, b):
    return lax.dot_general(a, b, (((0,), (0,)), ((), ())), preferred_element_type=F32)


def _rms_norm(x, gain):
    return x * lax.rsqrt(jnp.mean(x * x, axis=-1, keepdims=True) + RMS_EPS) * gain


def _shift_rows(u, tail, k):
    n = u.shape[0]
    ext = jnp.concatenate([tail, u], axis=0)
    return ext[SUBLANES - k:SUBLANES - k + n]


def _hy_mixer_kernel(x_ref, pos_ref, invf_ref, norm_ref, w_in_ref, qn_ref, kn_ref, sink_ref, hsum_ref,
                     conv_w_ref, w_out_ref, o_ref, k_tail, v_tail, y_tail, attn_buf, s_buf, p_buf, inv_buf):
    ti = pl.program_id(1)
    ts = x_ref.shape[1]

    @pl.when(ti == 0)
    def _():
        k_tail[...] = jnp.zeros_like(k_tail)
        v_tail[...] = jnp.zeros_like(v_tail)
        y_tail[...] = jnp.zeros_like(y_tail)

    x = x_ref[0]
    h = _rms_norm(x, norm_ref[...]).astype(BF16)

    pos = pos_ref[0].astype(F32)
    n_freq = HEAD_DIM // 2
    ang = invf_ref[0:n_freq, 0:1] * pos
    cos_t = jnp.concatenate([jnp.cos(ang)] * (LANES // n_freq), axis=0).T
    sin_t = jnp.concatenate([jnp.sin(ang)] * (LANES // n_freq), axis=0).T
    lane = lax.broadcasted_iota(jnp.int32, (1, LANES), 1)
    first_half = (lane % HEAD_DIM) < (HEAD_DIM // 2)
    sin_t = jnp.where(first_half, -sin_t, sin_t)

    def qk_norm_rope(t, gain_ref):
        w = t.shape[1]
        msq = _dot((t * t).astype(BF16), hsum_ref[0:w, 0:w])
        t = t * lax.rsqrt(msq + RMS_EPS) * gain_ref[...]
        reps = w // LANES
        cos_w = jnp.concatenate([cos_t] * reps, axis=1) if reps > 1 else cos_t
        sin_w = jnp.concatenate([sin_t] * reps, axis=1) if reps > 1 else sin_t
        fh = jnp.concatenate([first_half] * reps, axis=1) if reps > 1 else first_half
        half = HEAD_DIM // 2
        rot = jnp.where(fh, pltpu.roll(t, w - half, axis=1), pltpu.roll(t, half, axis=1))
        return t * cos_w + rot * sin_w

    q = _dot(h, w_in_ref[:, 0:ATTN_WIDTH])
    q = qk_norm_rope(q, qn_ref) * (HEAD_DIM ** -0.5)
    k = _dot(h, w_in_ref[:, ATTN_WIDTH:ATTN_WIDTH + KV_WIDTH])
    k = qk_norm_rope(k, kn_ref)
    v = _dot(h, w_in_ref[:, ATTN_WIDTH + KV_WIDTH:ATTN_WIDTH + 2 * KV_WIDTH])

    k_ext = jnp.concatenate([k_tail[...], k], axis=0)
    v_ext = jnp.concatenate([v_tail[...], v], axis=0)
    k_tail[...] = k[ts - ATTN_BLOCK:]
    v_tail[...] = v[ts - ATTN_BLOCK:]

    low = lane < HEAD_DIM
    k_sw = pltpu.roll(k_ext, HEAD_DIM, axis=1)
    v_sw = pltpu.roll(v_ext, HEAD_DIM, axis=1)
    zero = jnp.zeros_like(k_ext)
    k_lo = [jnp.where(low, k_ext, zero).astype(BF16), jnp.where(low, k_sw, zero).astype(BF16)]
    k_hi = [jnp.where(low, zero, k_sw).astype(BF16), jnp.where(low, zero, k_ext).astype(BF16)]
    v_lo = [jnp.where(low, v_ext, zero).astype(BF16), jnp.where(low, v_sw, zero).astype(BF16)]
    v_hi = [jnp.where(low, zero, v_sw).astype(BF16), jnp.where(low, zero, v_ext).astype(BF16)]

    n_blocks = ts // ATTN_BLOCK
    pair = 2 * ATTN_BLOCK
    qi = lax.broadcasted_iota(jnp.int32, (pair, pair), 0) % ATTN_BLOCK
    kj = lax.broadcasted_iota(jnp.int32, (pair, pair), 1)
    band = (kj > qi) & (kj <= qi + ATTN_BLOCK)
    band_first = band & (kj >= jnp.where(ti > 0, 0, ATTN_BLOCK))
    upper_rows = lax.broadcasted_iota(jnp.int32, (pair, 1), 0) >= ATTN_BLOCK

    q_bf = q.astype(BF16)
    for jb in range(n_blocks):
        r0 = jb * ATTN_BLOCK
        for g in range(ATTN_KV_HEADS):
            qs = jnp.concatenate([q_bf[r0:r0 + ATTN_BLOCK, (2 * g + i) * LANES:(2 * g + i + 1) * LANES]
                                  for i in range(2)], axis=0)
            kk = jnp.concatenate([k_lo[g][r0:r0 + pair], k_hi[g][r0:r0 + pair]], axis=0)
            s_buf[jb, g] = _dot_nt(qs, kk)

    c0 = ATTN_WIDTH + 2 * KV_WIDTH
    cb = _dot(h, w_in_ref[:, c0:c0 + CONV_WIDTH])
    cc = _dot(h, w_in_ref[:, c0 + CONV_WIDTH:c0 + 2 * CONV_WIDTH])
    cx = _dot(h, w_in_ref[:, c0 + 2 * CONV_WIDTH:c0 + 3 * CONV_WIDTH])

    for jb in range(n_blocks):
        mask = band_first if jb == 0 else band
        for g in range(ATTN_KV_HEADS):
            for half in range(2):
                s = jnp.where(mask, s_buf[jb, g, :, half * pair:(half + 1) * pair], NEG_BIG)
                sk = jnp.where(upper_rows, sink_ref[4 * g + 2 + half], sink_ref[4 * g + half])
                m = jnp.maximum(jnp.max(s, axis=-1, keepdims=True), sk)
                p = jnp.exp(s - m)
                inv_buf[jb, g, half] = 1.0 / (jnp.sum(p, axis=-1, keepdims=True) + jnp.exp(sk - m))
                p_buf[jb, g, :, half * pair:(half + 1) * pair] = p.astype(BF16)

    for jb in range(n_blocks):
        r0 = jb * ATTN_BLOCK
        for g in range(ATTN_KV_HEADS):
            vv = jnp.concatenate([v_lo[g][r0:r0 + pair], v_hi[g][r0:r0 + pair]], axis=0)
            o = _dot(p_buf[jb, g], vv)
            o = o * jnp.where(low, inv_buf[jb, g, 0], inv_buf[jb, g, 1])
            for i in range(2):
                attn_buf[r0:r0 + ATTN_BLOCK, (2 * g + i) * LANES:(2 * g + i + 1) * LANES] = (
                    o[i * ATTN_BLOCK:(i + 1) * ATTN_BLOCK].astype(BF16))

    y = cc * cx
    tail = y_tail[...]
    conv = (conv_w_ref[2:3, :] * y + conv_w_ref[1:2, :] * _shift_rows(y, tail, 1)
            + conv_w_ref[0:1, :] * _shift_rows(y, tail, 2))
    y_tail[...] = y[ts - SUBLANES:]
    gated = (cb * conv).astype(BF16)

    out = _dot(attn_buf[...], w_out_ref[0:ATTN_WIDTH, :]) + _dot(gated, w_out_ref[ATTN_WIDTH:, :])
    o_ref[0] = x + out


def _hy_mixer(x, positions, norm, w_in, q_norm, k_norm, sinks, conv_w, w_out):
    b, s, d = x.shape
    ts = SEQ_TILE
    inv_freq = ROPE_THETA ** (-jnp.arange(0, HEAD_DIM, 2, dtype=F32) / HEAD_DIM)
    invf = jnp.broadcast_to(jnp.tile(inv_freq, LANES // (HEAD_DIM // 2))[:, None], (LANES, LANES))
    head_id = jnp.arange(ATTN_WIDTH) // HEAD_DIM
    hsum = jnp.where(head_id[:, None] == head_id[None, :], 1.0 / HEAD_DIM, 0.0).astype(BF16)
    qn = jnp.tile(q_norm, ATTN_Q_HEADS)[None, :]
    kn = jnp.tile(k_norm, ATTN_KV_HEADS)[None, :]
    const = lambda *shape: pl.BlockSpec(shape, lambda bi, ti: (0,) * len(shape),
                                        pipeline_mode=pl.Buffered(1))
    return pl.pallas_call(
        _hy_mixer_kernel,
        grid=(b, s // ts),
        in_specs=[
            pl.BlockSpec((1, ts, d), lambda bi, ti: (bi, ti, 0)),
            pl.BlockSpec((1, 1, ts), lambda bi, ti: (bi, 0, ti)),
            const(LANES, LANES),
            const(1, d),
            const(d, HY_IN_WIDTH),
            const(1, ATTN_WIDTH),
            const(1, KV_WIDTH),
            pl.BlockSpec(memory_space=pltpu.SMEM),
            const(ATTN_WIDTH, ATTN_WIDTH),
            const(3, CONV_WIDTH),
            const(d, d),
        ],
        out_specs=pl.BlockSpec((1, ts, d), lambda bi, ti: (bi, ti, 0)),
        out_shape=jax.ShapeDtypeStruct(x.shape, x.dtype),
        scratch_shapes=[
            pltpu.VMEM((ATTN_BLOCK, KV_WIDTH), F32),
            pltpu.VMEM((ATTN_BLOCK, KV_WIDTH), F32),
            pltpu.VMEM((SUBLANES, CONV_WIDTH), F32),
            pltpu.VMEM((ts, ATTN_WIDTH), BF16),
            pltpu.VMEM((ts // ATTN_BLOCK, ATTN_KV_HEADS, 2 * ATTN_BLOCK, 4 * ATTN_BLOCK), F32),
            pltpu.VMEM((ts // ATTN_BLOCK, ATTN_KV_HEADS, 2 * ATTN_BLOCK, 4 * ATTN_BLOCK), BF16),
            pltpu.VMEM((ts // ATTN_BLOCK, ATTN_KV_HEADS, 2, 2 * ATTN_BLOCK, 1), F32),
        ],
        compiler_params=pltpu.CompilerParams(
            dimension_semantics=("arbitrary", "arbitrary"), vmem_limit_bytes=VMEM_LIMIT),
        name="hy_mixer",
    )(x, positions.reshape(b, 1, s), invf, norm[None, :], w_in.astype(BF16), qn, kn, sinks, hsum,
      conv_w, w_out.astype(BF16))


def _gla_mixer_kernel(x_ref, norm_ref, w_qkv_ref, w_glr_ref, w_og_ref, w_gup_ref, gbias_ref, tril_ref,
                      onorm_ref, w_out_ref, o_ref, state, u_buf, s_bf, sc_buf, o_buf):
    ti = pl.program_id(1)
    ts = x_ref.shape[1]
    c = GLA_CHUNK
    nc = ts // c
    nsub = c // GLA_SUB
    mid = GLA_SUB // 2
    kd = GLA_KEY_DIM

    @pl.when(ti == 0)
    def _():
        state[...] = jnp.zeros_like(state)

    x = x_ref[0]
    h = _rms_norm(x, norm_ref[...]).astype(BF16)

    g_lr = _dot(h, w_glr_ref[...])
    z = _dot(g_lr.astype(BF16), w_gup_ref[...]) + gbias_ref[...]
    log_a = (jnp.minimum(z, 0.0) - jnp.log1p(jnp.exp(-jnp.abs(z)))) / GLA_GATE_NORMALIZER
    la_hi = log_a.astype(BF16)
    la_lo = (log_a - la_hi.astype(F32)).astype(BF16)
    tril = tril_ref[...]
    tb = tril.shape[0]
    gc = jnp.concatenate([_dot(tril, la_hi[r:r + tb]) + _dot(tril, la_lo[r:r + tb]) for r in range(0, ts, tb)],
                         axis=0)

    q_all = _dot(h, w_qkv_ref[:, 0:kd]) * (GLA_DK ** -0.5)
    k_all = _dot(h, w_qkv_ref[:, kd:2 * kd])
    v_all = _dot(h, w_qkv_ref[:, 2 * kd:]).astype(BF16)

    g3 = gc.reshape(nc, c, kd)
    q3 = q_all.reshape(nc, c, kd)
    k3 = k_all.reshape(nc, c, kd)
    g_end = g3[:, c - 1:c, :]
    qe = (q3 * jnp.exp(g3)).astype(BF16)
    k_dec = (k3 * jnp.exp(g_end - g3)).astype(BF16)
    e_end = jnp.exp(g_end)
    g_refs = [g3[:, sb * GLA_SUB + mid:sb * GLA_SUB + mid + 1, :] for sb in range(nsub)]
    g_own = jnp.concatenate([jnp.broadcast_to(r, (nc, GLA_SUB, kd)) for r in g_refs], axis=1)
    qt = (q3 * jnp.exp(g3 - g_own)).astype(BF16)
    kts = []
    for sb in range(nsub):
        n = (sb + 1) * GLA_SUB
        kts.append((k3[:, 0:n, :] * jnp.exp(g_refs[sb] - g3[:, 0:n, :])).astype(BF16))

    ri = lax.broadcasted_iota(jnp.int32, (c, c), 0)
    cj = lax.broadcasted_iota(jnp.int32, (c, c), 1)
    causal = cj <= ri
    zq = jnp.zeros((GLA_SUB, GLA_DK), BF16)

    for ci in range(nc):
        r0 = ci * c
        for hd in range(GLA_HEADS):
            ks = slice(hd * GLA_DK, (hd + 1) * GLA_DK)
            vh = v_all[r0:r0 + c, hd * GLA_DV:(hd + 1) * GLA_DV]
            q_cols, k_cols = [], []
            for sb in range(nsub):
                s0 = sb * GLA_SUB
                n = s0 + GLA_SUB
                q_cols.append(jnp.concatenate([zq] * sb + [qt[ci, s0:n, ks]] + [zq] * (nsub - 1 - sb), axis=0))
                k_cols.append(jnp.concatenate([kts[sb][ci, :, ks]] + [zq] * (nsub - 1 - sb), axis=0))
            sc = _dot_nt(jnp.concatenate(q_cols, axis=1), jnp.concatenate(k_cols, axis=1))
            sc_buf[ci, hd] = jnp.where(causal, sc, 0.0).astype(BF16)
            u_buf[ci, hd] = _dot_tn(k_dec[ci, :, ks], vh)

    e_cols = e_end.reshape(nc, kd).T
    for hd in range(GLA_HEADS):
        st = state[hd]
        for ci in range(nc):
            s_bf[ci, hd] = st.astype(BF16)
            st = e_cols[hd * GLA_DK:(hd + 1) * GLA_DK, ci:ci + 1] * st + u_buf[ci, hd]
        state[hd] = st

    for ci in range(nc):
        r0 = ci * c
        for hd in range(GLA_HEADS):
            ks = slice(hd * GLA_DK, (hd + 1) * GLA_DK)
            vh = v_all[r0:r0 + c, hd * GLA_DV:(hd + 1) * GLA_DV]
            lhs = jnp.concatenate([qe[ci, :, ks], sc_buf[ci, hd]], axis=1)
            rhs = jnp.concatenate([s_bf[ci, hd], vh], axis=0)
            o_buf[r0:r0 + c, hd * GLA_DV:(hd + 1) * GLA_DV] = _dot(lhs, rhs)
    og = _dot(h, w_og_ref[...])
    gate = og * jax.nn.sigmoid(og)
    parts = []
    for hd in range(GLA_HEADS):
        vs = slice(hd * GLA_DV, (hd + 1) * GLA_DV)
        o = o_buf[:, vs]
        o = o * lax.rsqrt(jnp.mean(o * o, axis=-1, keepdims=True) + RMS_EPS) * onorm_ref[...]
        parts.append((o * gate[:, vs]).astype(BF16))
    o_ref[0] = x + _dot(jnp.concatenate(parts, axis=1), w_out_ref[...])


def _gla_mixer(x, norm, w_in, w_gate_up, gate_bias, o_norm, w_out):
    b, s, d = x.shape
    ts = SEQ_TILE
    c_qkv = 2 * GLA_KEY_DIM + GLA_VAL_DIM
    w_qkv = w_in[:, :c_qkv].astype(BF16)
    w_glr = jnp.pad(w_in[:, c_qkv:c_qkv + GLA_GATE_RANK], ((0, 0), (0, LANES - GLA_GATE_RANK))).astype(BF16)
    w_og = w_in[:, c_qkv + GLA_GATE_RANK:].astype(BF16)
    w_gup = jnp.pad(w_gate_up, ((0, LANES - GLA_GATE_RANK), (0, 0))).astype(BF16)
    rows = jnp.arange(GLA_TRIL_ROWS)
    tril = ((rows[:, None] >= rows[None, :]) & (rows[:, None] // GLA_CHUNK == rows[None, :] // GLA_CHUNK)).astype(BF16)
    const = lambda *shape: pl.BlockSpec(shape, lambda bi, ti: (0,) * len(shape),
                                        pipeline_mode=pl.Buffered(1))
    return pl.pallas_call(
        _gla_mixer_kernel,
        grid=(b, s // ts),
        in_specs=[
            pl.BlockSpec((1, ts, d), lambda bi, ti: (bi, ti, 0)),
            const(1, d),
            const(d, c_qkv),
            const(d, LANES),
            const(d, GLA_VAL_DIM),
            const(LANES, GLA_KEY_DIM),
            const(1, GLA_KEY_DIM),
            const(GLA_TRIL_ROWS, GLA_TRIL_ROWS),
            const(1, GLA_DV),
            const(GLA_VAL_DIM, d),
        ],
        out_specs=pl.BlockSpec((1, ts, d), lambda bi, ti: (bi, ti, 0)),
        out_shape=jax.ShapeDtypeStruct(x.shape, x.dtype),
        scratch_shapes=[
            pltpu.VMEM((GLA_HEADS, GLA_DK, GLA_DV), F32),
            pltpu.VMEM((ts // GLA_CHUNK, GLA_HEADS, GLA_DK, GLA_DV), F32),
            pltpu.VMEM((ts // GLA_CHUNK, GLA_HEADS, GLA_DK, GLA_DV), BF16),
            pltpu.VMEM((ts // GLA_CHUNK, GLA_HEADS, GLA_CHUNK, GLA_CHUNK), BF16),
            pltpu.VMEM((ts, GLA_VAL_DIM), F32),
        ],
        compiler_params=pltpu.CompilerParams(
            dimension_semantics=("arbitrary", "arbitrary"), vmem_limit_bytes=VMEM_LIMIT),
        name="gla_mixer",
    )(x, norm[None, :], w_qkv, w_glr, w_og, w_gup, gate_bias[None, :], tril, o_norm[None, :],
      w_out.astype(BF16))


def _ffn_ple_kernel(x_ref, p_ref, fnorm_ref, w_up_ref, conv_w_ref, conv_b_ref, w_down_ref, pnorm_ref,
                    w_gate_ref, w_proj_ref, o_ref, s0_tail, s1_tail, act_buf):
    ti = pl.program_id(1)
    ts = x_ref.shape[1]

    @pl.when(ti == 0)
    def _():
        s0_tail[...] = jnp.zeros_like(s0_tail)
        s1_tail[...] = jnp.zeros_like(s1_tail)

    rb = FFN_ROW_BLOCK
    n_chunks = D_FF // FF_CHUNK

    def conv_cols(u, c0):
        cols = slice(c0, c0 + FF_CHUNK)
        s0 = conv_w_ref[0:1, cols] * u
        s1 = conv_w_ref[1:2, cols] * u + _shift_rows(s0, s0_tail[:, cols], 1)
        out = conv_w_ref[2:3, cols] * u + _shift_rows(s1, s1_tail[:, cols], 1) + conv_b_ref[:, cols]
        s0_tail[:, cols] = s0[rb - SUBLANES:]
        s1_tail[:, cols] = s1[rb - SUBLANES:]
        return out

    for r0 in range(0, ts, rb):
        x = x_ref[0, r0:r0 + rb, :]
        h = _rms_norm(x, fnorm_ref[...]).astype(BF16)

        def up_dots(ci):
            return (_dot(h, w_up_ref[:, ci * FF_CHUNK:(ci + 1) * FF_CHUNK]),
                    _dot(h, w_up_ref[:, D_FF + ci * FF_CHUNK:D_FF + (ci + 1) * FF_CHUNK]))

        nxt = up_dots(0)
        for ci in range(n_chunks):
            u_gate, u_up = nxt
            if ci + 1 < n_chunks:
                nxt = up_dots(ci + 1)
            gate = conv_cols(u_gate, ci * FF_CHUNK)
            up = conv_cols(u_up, D_FF + ci * FF_CHUNK)
            gelu = 0.5 * gate * (1.0 + lax.erf(gate * (2.0 ** -0.5)))
            act_buf[r0:r0 + rb, ci * FF_CHUNK:(ci + 1) * FF_CHUNK] = (gelu * up).astype(BF16)
        x1 = x + _dot(act_buf[r0:r0 + rb, :], w_down_ref[...])

        hp = _rms_norm(x1, pnorm_ref[...]).astype(BF16)
        gate = jax.nn.sigmoid(_dot(hp, w_gate_ref[...]))
        emb = _dot(p_ref[0, r0:r0 + rb, :].astype(BF16), w_proj_ref[...])
        o_ref[0, r0:r0 + rb, :] = x1 + gate * emb


def _ffn_ple(x, p, layer, fnorm, w_up, conv_w, conv_b, w_down, pnorm, w_gate, w_proj):
    b, s, d = x.shape
    ts = FFN_SEQ_TILE
    const = lambda *shape: pl.BlockSpec(shape, lambda bi, ti: (0,) * len(shape),
                                        pipeline_mode=pl.Buffered(1))
    return pl.pallas_call(
        _ffn_ple_kernel,
        grid=(b, s // ts),
        in_specs=[
            pl.BlockSpec((1, ts, d), lambda bi, ti: (bi, ti, 0)),
            pl.BlockSpec((None, 1, ts, PLE_DIM), lambda bi, ti: (layer, bi, ti, 0)),
            const(1, d),
            const(d, 2 * D_FF),
            const(3, 2 * D_FF),
            const(1, 2 * D_FF),
            const(D_FF, d),
            const(1, d),
            const(d, d),
            const(PLE_DIM, d),
        ],
        out_specs=pl.BlockSpec((1, ts, d), lambda bi, ti: (bi, ti, 0)),
        out_shape=jax.ShapeDtypeStruct(x.shape, x.dtype),
        scratch_shapes=[pltpu.VMEM((SUBLANES, 2 * D_FF), F32), pltpu.VMEM((SUBLANES, 2 * D_FF), F32),
                        pltpu.VMEM((ts, D_FF), BF16)],
        compiler_params=pltpu.CompilerParams(
            dimension_semantics=("arbitrary", "arbitrary"), vmem_limit_bytes=VMEM_LIMIT),
        name="ffn_ple",
    )(x, p, fnorm[None, :], w_up.astype(BF16), conv_w, conv_b[None, :], w_down.astype(BF16),
      pnorm[None, :], w_gate.astype(BF16), w_proj.astype(BF16))


def kernel(x, p, positions, mix_norm, ffn_norm, ffn_w_up, ffn_conv_w, ffn_conv_b, ffn_w_down, ple_norm,
           ple_w_gate, ple_w_proj, hy_w_in, hy_q_norm, hy_k_norm, hy_sinks, hy_conv_w, hy_w_out, gla_w_in,
           gla_w_gate_up, gla_gate_bias, gla_o_norm, gla_w_out):
    depth = mix_norm.shape[0]
    for i in range(depth):
        j = i // 2
        if i % 2 == 0:
            x = _hy_mixer(x, positions, mix_norm[i], hy_w_in[j], hy_q_norm[j], hy_k_norm[j], hy_sinks[j],
                          hy_conv_w[j], hy_w_out[j])
        else:
            x = _gla_mixer(x, mix_norm[i], gla_w_in[j], gla_w_gate_up[j], gla_gate_bias[j], gla_o_norm[j],
                           gla_w_out[j])
        x = _ffn_ple(x, p, i, ffn_norm[i], ffn_w_up[i], ffn_conv_w[i], ffn_conv_b[i], ffn_w_down[i],
                     ple_norm[i], ple_w_gate[i], ple_w_proj[i])
    return x
```

```python
import functools

import jax
import jax.numpy as jnp
from jax import lax
from jax.experimental import pallas as pl
from jax.experimental.pallas import tpu as pltpu

D_MODEL = 1024
HEAD_DIM = 64
ATTN_Q_HEADS = 8
ATTN_KV_HEADS = 2
ATTN_BLOCK = 128
ROPE_THETA = 10000.0
ATTN_WIDTH = ATTN_Q_HEADS * HEAD_DIM
KV_WIDTH = ATTN_KV_HEADS * HEAD_DIM
CONV_WIDTH = D_MODEL // 2
HY_IN_WIDTH = ATTN_WIDTH + 2 * KV_WIDTH + 3 * CONV_WIDTH
GLA_HEADS = 4
GLA_KEY_DIM = D_MODEL // 2
GLA_VAL_DIM = D_MODEL
GLA_DK = GLA_KEY_DIM // GLA_HEADS
GLA_DV = GLA_VAL_DIM // GLA_HEADS
GLA_GATE_RANK = 16
GLA_GATE_NORMALIZER = 16.0
GLA_CHUNK = 64
GLA_TRIL_ROWS = 256
GLA_SUB = 16
D_FF = 2816
PLE_DIM = 256
RMS_EPS = 1e-6

LANES = 128
SUBLANES = 8
SEQ_TILE = 512
FFN_SEQ_TILE = 512
FFN_ROW_BLOCK = 256
FF_CHUNK = 256
VMEM_LIMIT = 56 * 1024 * 1024
NEG_BIG = -1e30
LOG2_E = 1.4426950408889634

F32 = jnp.float32
BF16 = jnp.bfloat16


def _dot(a, b):
    return jnp.dot(a, b, preferred_element_type=F32)


def _dot_nt(a, b):
    return lax.dot_general(a, b, (((1,), (1,)), ((), ())), preferred_element_type=F32)


def _dot_tn(a, b):
    return lax.dot_general(a, b, (((0,), (0,)), ((), ())), preferred_element_type=F32)


def _rms_norm(x, gain):
    return x * lax.rsqrt(jnp.mean(x * x, axis=-1, keepdims=True) + RMS_EPS) * gain


def _shift_rows(u, tail, k):
    n = u.shape[0]
    ext = jnp.concatenate([tail, u], axis=0)
    return ext[SUBLANES - k:SUBLANES - k + n]


def _hy_mixer_kernel(x_ref, pos_ref, invf_ref, norm_ref, w_in_ref, qn_ref, kn_ref, sink_ref, hsum_ref,
                     conv_w_ref, w_out_ref, o_ref, k_tail, v_tail, y_tail, attn_buf):
    ti = pl.program_id(1)
    ts = x_ref.shape[1]

    @pl.when(ti == 0)
    def _():
        k_tail[...] = jnp.zeros_like(k_tail)
        v_tail[...] = jnp.zeros_like(v_tail)
        y_tail[...] = jnp.zeros_like(y_tail)

    x = x_ref[0]
    h = _rms_norm(x, norm_ref[...]).astype(BF16)

    pos = pos_ref[0].astype(F32)
    n_freq = HEAD_DIM // 2
    ang = invf_ref[0:n_freq, 0:1] * pos
    cos_t = jnp.concatenate([jnp.cos(ang)] * (LANES // n_freq), axis=0).T
    sin_t = jnp.concatenate([jnp.sin(ang)] * (LANES // n_freq), axis=0).T
    lane = lax.broadcasted_iota(jnp.int32, (1, LANES), 1)
    first_half = (lane % HEAD_DIM) < (HEAD_DIM // 2)
    sin_t = jnp.where(first_half, -sin_t, sin_t)

    def qk_norm_rope(t, gain_ref):
        w = t.shape[1]
        msq = _dot((t * t).astype(BF16), hsum_ref[0:w, 0:w])
        t = t * lax.rsqrt(msq + RMS_EPS) * gain_ref[...]
        reps = w // LANES
        cos_w = jnp.concatenate([cos_t] * reps, axis=1) if reps > 1 else cos_t
        sin_w = jnp.concatenate([sin_t] * reps, axis=1) if reps > 1 else sin_t
        fh = jnp.concatenate([first_half] * reps, axis=1) if reps > 1 else first_half
        half = HEAD_DIM // 2
        rot = jnp.where(fh, pltpu.roll(t, w - half, axis=1), pltpu.roll(t, half, axis=1))
        return t * cos_w + rot * sin_w

    q = _dot(h, w_in_ref[:, 0:ATTN_WIDTH])
    q = qk_norm_rope(q, qn_ref) * (HEAD_DIM ** -0.5 * LOG2_E)
    k = _dot(h, w_in_ref[:, ATTN_WIDTH:ATTN_WIDTH + KV_WIDTH])
    k = qk_norm_rope(k, kn_ref)
    v = _dot(h, w_in_ref[:, ATTN_WIDTH + KV_WIDTH:ATTN_WIDTH + 2 * KV_WIDTH])

    k_ext = jnp.concatenate([k_tail[...], k], axis=0)
    v_ext = jnp.concatenate([v_tail[...], v], axis=0)
    k_tail[...] = k[ts - ATTN_BLOCK:]
    v_tail[...] = v[ts - ATTN_BLOCK:]

    low = lane < HEAD_DIM
    k_sw = pltpu.roll(k_ext, HEAD_DIM, axis=1)
    v_sw = pltpu.roll(v_ext, HEAD_DIM, axis=1)
    zero = jnp.zeros_like(k_ext)
    k_lo = [jnp.where(low, k_ext, zero).astype(BF16), jnp.where(low, k_sw, zero).astype(BF16)]
    k_hi = [jnp.where(low, zero, k_sw).astype(BF16), jnp.where(low, zero, k_ext).astype(BF16)]
    v_lo = [jnp.where(low, v_ext, zero).astype(BF16), jnp.where(low, v_sw, zero).astype(BF16)]
    v_hi = [jnp.where(low, zero, v_sw).astype(BF16), jnp.where(low, zero, v_ext).astype(BF16)]

    n_blocks = ts // ATTN_BLOCK
    pair = 2 * ATTN_BLOCK
    qi = lax.broadcasted_iota(jnp.int32, (pair, pair), 0) % ATTN_BLOCK
    kj = lax.broadcasted_iota(jnp.int32, (pair, pair), 1)
    band = (kj > qi) & (kj <= qi + ATTN_BLOCK)
    band_first = band & (kj >= jnp.where(ti > 0, 0, ATTN_BLOCK))
    upper_rows = lax.broadcasted_iota(jnp.int32, (pair, 1), 0) >= ATTN_BLOCK

    q_bf = q.astype(BF16)
    units = [(jb, g) for jb in range(n_blocks) for g in range(ATTN_KV_HEADS)]

    def scores(jb, g):
        r0 = jb * ATTN_BLOCK
        qs = jnp.concatenate([q_bf[r0:r0 + ATTN_BLOCK, (2 * g + i) * LANES:(2 * g + i + 1) * LANES]
                              for i in range(2)], axis=0)
        kk = jnp.concatenate([k_lo[g][r0:r0 + pair], k_hi[g][r0:r0 + pair]], axis=0)
        return _dot_nt(qs, kk)

    c0 = ATTN_WIDTH + 2 * KV_WIDTH
    conv_starts = [c0 + i * 2 * LANES for i in range(3 * CONV_WIDTH // (2 * LANES))]
    conv_parts = []
    s_next = scores(*units[0])
    for ui, (jb, g) in enumerate(units):
        r0 = jb * ATTN_BLOCK
        s_cur = s_next
        if ui + 1 < len(units):
            s_next = scores(*units[ui + 1])
        if conv_starts:
            cs = conv_starts.pop(0)
            conv_parts.append(_dot(h, w_in_ref[:, cs:cs + 2 * LANES]))
        mask = band_first if jb == 0 else band
        probs, invs = [], []
        for half in range(2):
            s = jnp.where(mask, s_cur[:, half * pair:(half + 1) * pair], NEG_BIG)
            sk = jnp.where(upper_rows, sink_ref[4 * g + 2 + half], sink_ref[4 * g + half]) * LOG2_E
            m = jnp.maximum(jnp.max(s, axis=-1, keepdims=True), sk)
            p = jnp.exp2(s - m)
            invs.append(1.0 / (jnp.sum(p, axis=-1, keepdims=True) + jnp.exp2(sk - m)))
            probs.append(p.astype(BF16))
        vv = jnp.concatenate([v_lo[g][r0:r0 + pair], v_hi[g][r0:r0 + pair]], axis=0)
        o = _dot(jnp.concatenate(probs, axis=1), vv)
        o = o * jnp.where(low, invs[0], invs[1])
        for i in range(2):
            attn_buf[r0:r0 + ATTN_BLOCK, (2 * g + i) * LANES:(2 * g + i + 1) * LANES] = (
                o[i * ATTN_BLOCK:(i + 1) * ATTN_BLOCK].astype(BF16))
    cb = jnp.concatenate(conv_parts[0:2], axis=1)
    cc = jnp.concatenate(conv_parts[2:4], axis=1)
    cx = jnp.concatenate(conv_parts[4:6], axis=1)

    y = cc * cx
    tail = y_tail[...]
    conv = (conv_w_ref[2:3, :] * y + conv_w_ref[1:2, :] * _shift_rows(y, tail, 1)
            + conv_w_ref[0:1, :] * _shift_rows(y, tail, 2))
    y_tail[...] = y[ts - SUBLANES:]
    gated = (cb * conv).astype(BF16)

    out = _dot(attn_buf[...], w_out_ref[0:ATTN_WIDTH, :]) + _dot(gated, w_out_ref[ATTN_WIDTH:, :])
    o_ref[0] = x + out


def _hy_mixer(x, positions, norm, w_in, q_norm, k_norm, sinks, conv_w, w_out):
    b, s, d = x.shape
    ts = SEQ_TILE
    inv_freq = ROPE_THETA ** (-jnp.arange(0, HEAD_DIM, 2, dtype=F32) / HEAD_DIM)
    invf = jnp.broadcast_to(jnp.tile(inv_freq, LANES // (HEAD_DIM // 2))[:, None], (LANES, LANES))
    head_id = jnp.arange(ATTN_WIDTH) // HEAD_DIM
    hsum = jnp.where(head_id[:, None] == head_id[None, :], 1.0 / HEAD_DIM, 0.0).astype(BF16)
    qn = jnp.tile(q_norm, ATTN_Q_HEADS)[None, :]
    kn = jnp.tile(k_norm, ATTN_KV_HEADS)[None, :]
    const = lambda *shape: pl.BlockSpec(shape, lambda bi, ti: (0,) * len(shape),
                                        pipeline_mode=pl.Buffered(1))
    return pl.pallas_call(
        _hy_mixer_kernel,
        grid=(b, s // ts),
        in_specs=[
            pl.BlockSpec((1, ts, d), lambda bi, ti: (bi, ti, 0)),
            pl.BlockSpec((1, 1, ts), lambda bi, ti: (bi, 0, ti)),
            const(LANES, LANES),
            const(1, d),
            const(d, HY_IN_WIDTH),
            const(1, ATTN_WIDTH),
            const(1, KV_WIDTH),
            pl.BlockSpec(memory_space=pltpu.SMEM),
            const(ATTN_WIDTH, ATTN_WIDTH),
            const(3, CONV_WIDTH),
            const(d, d),
        ],
        out_specs=pl.BlockSpec((1, ts, d), lambda bi, ti: (bi, ti, 0)),
        out_shape=jax.ShapeDtypeStruct(x.shape, x.dtype),
        scratch_shapes=[
            pltpu.VMEM((ATTN_BLOCK, KV_WIDTH), F32),
            pltpu.VMEM((ATTN_BLOCK, KV_WIDTH), F32),
            pltpu.VMEM((SUBLANES, CONV_WIDTH), F32),
            pltpu.VMEM((ts, ATTN_WIDTH), BF16),
        ],
        compiler_params=pltpu.CompilerParams(
            dimension_semantics=("arbitrary", "arbitrary"), vmem_limit_bytes=VMEM_LIMIT),
        name="hy_mixer",
    )(x, positions.reshape(b, 1, s), invf, norm[None, :], w_in.astype(BF16), qn, kn, sinks, hsum,
      conv_w, w_out.astype(BF16))


def _gla_mixer_kernel(x_ref, norm_ref, w_qkv_ref, w_glr_ref, w_og_ref, w_gup_ref, gbias_ref, tril_ref,
                      onorm_ref, w_out_ref, o_ref, state, u_buf, s_bf, sc_buf, o_buf):
    ti = pl.program_id(1)
    ts = x_ref.shape[1]
    c = GLA_CHUNK
    nc = ts // c
    nsub = c // GLA_SUB
    mid = GLA_SUB // 2
    kd = GLA_KEY_DIM

    @pl.when(ti == 0)
    def _():
        state[...] = jnp.zeros_like(state)

    x = x_ref[0]
    h = _rms_norm(x, norm_ref[...]).astype(BF16)

    g_lr = _dot(h, w_glr_ref[...])
    z = _dot(g_lr.astype(BF16), w_gup_ref[...]) + gbias_ref[...]
    log_a = (jnp.minimum(z, 0.0) - jnp.log(1.0 + jnp.exp(-jnp.abs(z)))) * (LOG2_E / GLA_GATE_NORMALIZER)
    la_hi = log_a.astype(BF16)
    la_lo = (log_a - la_hi.astype(F32)).astype(BF16)
    tril = tril_ref[...]
    tb = tril.shape[0]
    gc = jnp.concatenate([_dot(tril, la_hi[r:r + tb]) + _dot(tril, la_lo[r:r + tb]) for r in range(0, ts, tb)],
                         axis=0)

    q_all = _dot(h, w_qkv_ref[:, 0:kd]) * (GLA_DK ** -0.5)
    k_all = _dot(h, w_qkv_ref[:, kd:2 * kd])
    v_all = _dot(h, w_qkv_ref[:, 2 * kd:]).astype(BF16)

    g3 = gc.reshape(nc, c, kd)
    q3 = q_all.reshape(nc, c, kd)
    k3 = k_all.reshape(nc, c, kd)
    g_end = g3[:, c - 1:c, :]
    qe = (q3 * jnp.exp2(g3)).astype(BF16)
    k_dec = (k3 * jnp.exp2(g_end - g3)).astype(BF16)
    e_end = jnp.exp2(g_end)
    g_refs = [g3[:, sb * GLA_SUB + mid:sb * GLA_SUB + mid + 1, :] for sb in range(nsub)]
    g_own = jnp.concatenate([jnp.broadcast_to(r, (nc, GLA_SUB, kd)) for r in g_refs], axis=1)
    qt = (q3 * jnp.exp2(g3 - g_own)).astype(BF16)
    kts = []
    for sb in range(nsub):
        n = (sb + 1) * GLA_SUB
        kts.append((k3[:, 0:n, :] * jnp.exp2(g_refs[sb] - g3[:, 0:n, :])).astype(BF16))

    ri = lax.broadcasted_iota(jnp.int32, (c, c), 0)
    cj = lax.broadcasted_iota(jnp.int32, (c, c), 1)
    causal = cj <= ri
    zq = jnp.zeros((GLA_SUB, GLA_DK), BF16)

    for ci in range(nc):
        r0 = ci * c
        for hd in range(GLA_HEADS):
            ks = slice(hd * GLA_DK, (hd + 1) * GLA_DK)
            vh = v_all[r0:r0 + c, hd * GLA_DV:(hd + 1) * GLA_DV]
            q_cols, k_cols = [], []
            for sb in range(nsub):
                s0 = sb * GLA_SUB
                n = s0 + GLA_SUB
                q_cols.append(jnp.concatenate([zq] * sb + [qt[ci, s0:n, ks]] + [zq] * (nsub - 1 - sb), axis=0))
                k_cols.append(jnp.concatenate([kts[sb][ci, :, ks]] + [zq] * (nsub - 1 - sb), axis=0))
            sc = _dot_nt(jnp.concatenate(q_cols, axis=1), jnp.concatenate(k_cols, axis=1))
            sc_buf[ci, hd] = jnp.where(causal, sc, 0.0).astype(BF16)
            u_buf[ci, hd] = _dot_tn(k_dec[ci, :, ks], vh)

    e_cols = e_end.reshape(nc, kd).T
    for hd in range(GLA_HEADS):
        st = state[hd]
        for ci in range(nc):
            s_bf[ci, hd] = st.astype(BF16)
            st = e_cols[hd * GLA_DK:(hd + 1) * GLA_DK, ci:ci + 1] * st + u_buf[ci, hd]
        state[hd] = st

    for ci in range(nc):
        r0 = ci * c
        for hd in range(GLA_HEADS):
            ks = slice(hd * GLA_DK, (hd + 1) * GLA_DK)
            vh = v_all[r0:r0 + c, hd * GLA_DV:(hd + 1) * GLA_DV]
            lhs = jnp.concatenate([qe[ci, :, ks], sc_buf[ci, hd]], axis=1)
            rhs = jnp.concatenate([s_bf[ci, hd], vh], axis=0)
            o_buf[r0:r0 + c, hd * GLA_DV:(hd + 1) * GLA_DV] = _dot(lhs, rhs)
    og = _dot(h, w_og_ref[...])
    gate = og * jax.nn.sigmoid(og)
    parts = []
    for hd in range(GLA_HEADS):
        vs = slice(hd * GLA_DV, (hd + 1) * GLA_DV)
        o = o_buf[:, vs]
        o = o * lax.rsqrt(jnp.mean(o * o, axis=-1, keepdims=True) + RMS_EPS) * onorm_ref[...]
        parts.append((o * gate[:, vs]).astype(BF16))
    o_ref[0] = x + _dot(jnp.concatenate(parts, axis=1), w_out_ref[...])


def _gla_mixer(x, norm, w_in, w_gate_up, gate_bias, o_norm, w_out):
    b, s, d = x.shape
    ts = SEQ_TILE
    c_qkv = 2 * GLA_KEY_DIM + GLA_VAL_DIM
    w_qkv = w_in[:, :c_qkv].astype(BF16)
    w_glr = jnp.pad(w_in[:, c_qkv:c_qkv + GLA_GATE_RANK], ((0, 0), (0, LANES - GLA_GATE_RANK))).astype(BF16)
    w_og = w_in[:, c_qkv + GLA_GATE_RANK:].astype(BF16)
    w_gup = jnp.pad(w_gate_up, ((0, LANES - GLA_GATE_RANK), (0, 0))).astype(BF16)
    rows = jnp.arange(GLA_TRIL_ROWS)
    tril = ((rows[:, None] >= rows[None, :]) & (rows[:, None] // GLA_CHUNK == rows[None, :] // GLA_CHUNK)).astype(BF16)
    const = lambda *shape: pl.BlockSpec(shape, lambda bi, ti: (0,) * len(shape),
                                        pipeline_mode=pl.Buffered(1))
    return pl.pallas_call(
        _gla_mixer_kernel,
        grid=(b, s // ts),
        in_specs=[
            pl.BlockSpec((1, ts, d), lambda bi, ti: (bi, ti, 0)),
            const(1, d),
            const(d, c_qkv),
            const(d, LANES),
            const(d, GLA_VAL_DIM),
            const(LANES, GLA_KEY_DIM),
            const(1, GLA_KEY_DIM),
            const(GLA_TRIL_ROWS, GLA_TRIL_ROWS),
            const(1, GLA_DV),
            const(GLA_VAL_DIM, d),
        ],
        out_specs=pl.BlockSpec((1, ts, d), lambda bi, ti: (bi, ti, 0)),
        out_shape=jax.ShapeDtypeStruct(x.shape, x.dtype),
        scratch_shapes=[
            pltpu.VMEM((GLA_HEADS, GLA_DK, GLA_DV), F32),
            pltpu.VMEM((ts // GLA_CHUNK, GLA_HEADS, GLA_DK, GLA_DV), F32),
            pltpu.VMEM((ts // GLA_CHUNK, GLA_HEADS, GLA_DK, GLA_DV), BF16),
            pltpu.VMEM((ts // GLA_CHUNK, GLA_HEADS, GLA_CHUNK, GLA_CHUNK), BF16),
            pltpu.VMEM((ts, GLA_VAL_DIM), F32),
        ],
        compiler_params=pltpu.CompilerParams(
            dimension_semantics=("arbitrary", "arbitrary"), vmem_limit_bytes=VMEM_LIMIT),
        name="gla_mixer",
    )(x, norm[None, :], w_qkv, w_glr, w_og, w_gup, gate_bias[None, :], tril, o_norm[None, :],
      w_out.astype(BF16))


def _ffn_ple_kernel(x_ref, p_ref, fnorm_ref, w_up_ref, conv_w_ref, conv_b_ref, w_down_ref, pnorm_ref,
                    w_gate_ref, w_proj_ref, o_ref, s0_tail, s1_tail, act_buf):
    ti = pl.program_id(1)
    ts = x_ref.shape[1]

    @pl.when(ti == 0)
    def _():
        s0_tail[...] = jnp.zeros_like(s0_tail)
        s1_tail[...] = jnp.zeros_like(s1_tail)

    rb = FFN_ROW_BLOCK
    n_chunks = D_FF // FF_CHUNK

    def conv_cols(u, c0):
        cols = slice(c0, c0 + FF_CHUNK)
        s0 = conv_w_ref[0:1, cols] * u
        s1 = conv_w_ref[1:2, cols] * u + _shift_rows(s0, s0_tail[:, cols], 1)
        out = conv_w_ref[2:3, cols] * u + _shift_rows(s1, s1_tail[:, cols], 1) + conv_b_ref[:, cols]
        s0_tail[:, cols] = s0[rb - SUBLANES:]
        s1_tail[:, cols] = s1[rb - SUBLANES:]
        return out

    for r0 in range(0, ts, rb):
        x = x_ref[0, r0:r0 + rb, :]
        h = _rms_norm(x, fnorm_ref[...]).astype(BF16)

        def up_dots(ci):
            return (_dot(h, w_up_ref[:, ci * FF_CHUNK:(ci + 1) * FF_CHUNK]),
                    _dot(h, w_up_ref[:, D_FF + ci * FF_CHUNK:D_FF + (ci + 1) * FF_CHUNK]))

        nxt = up_dots(0)
        for ci in range(n_chunks):
            u_gate, u_up = nxt
            if ci + 1 < n_chunks:
                nxt = up_dots(ci + 1)
            gate = conv_cols(u_gate, ci * FF_CHUNK)
            up = conv_cols(u_up, D_FF + ci * FF_CHUNK)
            act_buf[r0:r0 + rb, ci * FF_CHUNK:(ci + 1) * FF_CHUNK] = (
                gate * (1.0 + lax.erf(gate)) * up).astype(BF16)
        x1 = x + _dot(act_buf[r0:r0 + rb, :], w_down_ref[...])

        hp = _rms_norm(x1, pnorm_ref[...]).astype(BF16)
        gate = jax.nn.sigmoid(_dot(hp, w_gate_ref[...]))
        emb = _dot(p_ref[0, r0:r0 + rb, :].astype(BF16), w_proj_ref[...])
        o_ref[0, r0:r0 + rb, :] = x1 + gate * emb


def _ffn_ple(x, p, layer, fnorm, w_up, conv_w, conv_b, w_down, pnorm, w_gate, w_proj):
    b, s, d = x.shape
    ts = FFN_SEQ_TILE
    col_scale = jnp.where(jnp.arange(2 * D_FF) < D_FF, 2.0 ** -0.5, 1.0).astype(F32)[None, :]
    const = lambda *shape: pl.BlockSpec(shape, lambda bi, ti: (0,) * len(shape),
                                        pipeline_mode=pl.Buffered(1))
    return pl.pallas_call(
        _ffn_ple_kernel,
        grid=(b, s // ts),
        in_specs=[
            pl.BlockSpec((1, ts, d), lambda bi, ti: (bi, ti, 0)),
            pl.BlockSpec((None, 1, ts, PLE_DIM), lambda bi, ti: (layer, bi, ti, 0)),
            const(1, d),
            const(d, 2 * D_FF),
            const(3, 2 * D_FF),
            const(1, 2 * D_FF),
            const(D_FF, d),
            const(1, d),
            const(d, d),
            const(PLE_DIM, d),
        ],
        out_specs=pl.BlockSpec((1, ts, d), lambda bi, ti: (bi, ti, 0)),
        out_shape=jax.ShapeDtypeStruct(x.shape, x.dtype),
        scratch_shapes=[pltpu.VMEM((SUBLANES, 2 * D_FF), F32), pltpu.VMEM((SUBLANES, 2 * D_FF), F32),
                        pltpu.VMEM((ts, D_FF), BF16)],
        compiler_params=pltpu.CompilerParams(
            dimension_semantics=("arbitrary", "arbitrary"), vmem_limit_bytes=VMEM_LIMIT),
        name="ffn_ple",
    )(x, p, fnorm[None, :], (w_up * col_scale).astype(BF16), conv_w, (conv_b * col_scale[0])[None, :],
      (w_down * (0.5 * 2.0 ** 0.5)).astype(BF16),
      pnorm[None, :], w_gate.astype(BF16), w_proj.astype(BF16))


def kernel(x, p, positions, mix_norm, ffn_norm, ffn_w_up, ffn_conv_w, ffn_conv_b, ffn_w_down, ple_norm,
           ple_w_gate, ple_w_proj, hy_w_in, hy_q_norm, hy_k_norm, hy_sinks, hy_conv_w, hy_w_out, gla_w_in,
           gla_w_gate_up, gla_gate_bias, gla_o_norm, gla_w_out):
    depth = mix_norm.shape[0]
    for i in range(depth):
        j = i // 2
        if i % 2 == 0:
            x = _hy_mixer(x, positions, mix_norm[i], hy_w_in[j], hy_q_norm[j], hy_k_norm[j], hy_sinks[j],
                          hy_conv_w[j], hy_w_out[j])
        else:
            x = _gla_mixer(x, mix_norm[i], gla_w_in[j], gla_w_gate_up[j], gla_gate_bias[j], gla_o_norm[j],
                           gla_w_out[j])
        x = _ffn_ple(x, p, i, ffn_norm[i], ffn_w_up[i], ffn_conv_w[i], ffn_conv_b[i], ffn_w_down[i],
                     ple_norm[i], ple_w_gate[i], ple_w_proj[i])
    return x
```

```python
import functools

import jax
import jax.numpy as jnp
from jax import lax
from jax.experimental import pallas as pl
from jax.experimental.pallas import tpu as pltpu

D_MODEL = 1024
HEAD_DIM = 64
ATTN_Q_HEADS = 8
ATTN_KV_HEADS = 2
ATTN_BLOCK = 128
ROPE_THETA = 10000.0
ATTN_WIDTH = ATTN_Q_HEADS * HEAD_DIM
KV_WIDTH = ATTN_KV_HEADS * HEAD_DIM
CONV_WIDTH = D_MODEL // 2
HY_IN_WIDTH = ATTN_WIDTH + 2 * KV_WIDTH + 3 * CONV_WIDTH
GLA_HEADS = 4
GLA_KEY_DIM = D_MODEL // 2
GLA_VAL_DIM = D_MODEL
GLA_DK = GLA_KEY_DIM // GLA_HEADS
GLA_DV = GLA_VAL_DIM // GLA_HEADS
GLA_GATE_RANK = 16
GLA_GATE_NORMALIZER = 16.0
GLA_CHUNK = 64
GLA_TRIL_ROWS = 256
GLA_SUB = 16
D_FF = 2816
PLE_DIM = 256
RMS_EPS = 1e-6

LANES = 128
SUBLANES = 8
SEQ_TILE = 512
FFN_SEQ_TILE = 512
FFN_ROW_BLOCK = 256
FF_CHUNK = 256
VMEM_LIMIT = 56 * 1024 * 1024
NEG_BIG = -1e30
LOG2_E = 1.4426950408889634

F32 = jnp.float32
BF16 = jnp.bfloat16


def _dot(a, b):
    return jnp.dot(a, b, preferred_element_type=F32)


def _dot_nt(a, b):
    return lax.dot_general(a, b, (((1,), (1,)), ((), ())), preferred_element_type=F32)


def _dot_tn(a, b):
    return lax.dot_general(a, b, (((0,), (0,)), ((), ())), preferred_element_type=F32)


def _rms_norm(x, gain):
    return x * lax.rsqrt(jnp.mean(x * x, axis=-1, keepdims=True) + RMS_EPS) * gain


def _shift_rows(u, tail, k):
    n = u.shape[0]
    ext = jnp.concatenate([tail, u], axis=0)
    return ext[SUBLANES - k:SUBLANES - k + n]


def _hy_mixer_kernel(x_ref, pos_ref, invf_ref, norm_ref, w_in_ref, qn_ref, kn_ref, sink_ref, hsum_ref,
                     conv_w_ref, w_out_ref, o_ref, k_tail, v_tail, y_tail, attn_buf):
    ti = pl.program_id(1)
    ts = x_ref.shape[1]

    @pl.when(ti == 0)
    def _():
        k_tail[...] = jnp.zeros_like(k_tail)
        v_tail[...] = jnp.zeros_like(v_tail)
        y_tail[...] = jnp.zeros_like(y_tail)

    x = x_ref[0]
    h = _rms_norm(x, norm_ref[...]).astype(BF16)

    pos = pos_ref[0].astype(F32)
    n_freq = HEAD_DIM // 2
    ang = invf_ref[0:n_freq, 0:1] * pos
    cos_t = jnp.concatenate([jnp.cos(ang)] * (LANES // n_freq), axis=0).T
    sin_t = jnp.concatenate([jnp.sin(ang)] * (LANES // n_freq), axis=0).T
    lane = lax.broadcasted_iota(jnp.int32, (1, LANES), 1)
    first_half = (lane % HEAD_DIM) < (HEAD_DIM // 2)
    sin_t = jnp.where(first_half, -sin_t, sin_t)

    def qk_norm_rope(t, gain_ref):
        w = t.shape[1]
        msq = _dot((t * t).astype(BF16), hsum_ref[0:w, 0:w])
        t = t * lax.rsqrt(msq + RMS_EPS) * gain_ref[...]
        reps = w // LANES
        cos_w = jnp.concatenate([cos_t] * reps, axis=1) if reps > 1 else cos_t
        sin_w = jnp.concatenate([sin_t] * reps, axis=1) if reps > 1 else sin_t
        fh = jnp.concatenate([first_half] * reps, axis=1) if reps > 1 else first_half
        half = HEAD_DIM // 2
        rot = jnp.where(fh, pltpu.roll(t, w - half, axis=1), pltpu.roll(t, half, axis=1))
        return t * cos_w + rot * sin_w

    q = _dot(h, w_in_ref[:, 0:ATTN_WIDTH])
    q = qk_norm_rope(q, qn_ref) * (HEAD_DIM ** -0.5 * LOG2_E)
    k = _dot(h, w_in_ref[:, ATTN_WIDTH:ATTN_WIDTH + KV_WIDTH])
    k = qk_norm_rope(k, kn_ref)
    v = _dot(h, w_in_ref[:, ATTN_WIDTH + KV_WIDTH:ATTN_WIDTH + 2 * KV_WIDTH])

    k_ext = jnp.concatenate([k_tail[...], k], axis=0)
    v_ext = jnp.concatenate([v_tail[...], v], axis=0)
    k_tail[...] = k[ts - ATTN_BLOCK:]
    v_tail[...] = v[ts - ATTN_BLOCK:]

    low = lane < HEAD_DIM
    k_sw = pltpu.roll(k_ext, HEAD_DIM, axis=1)
    v_sw = pltpu.roll(v_ext, HEAD_DIM, axis=1)
    zero = jnp.zeros_like(k_ext)
    k_lo = [jnp.where(low, k_ext, zero).astype(BF16), jnp.where(low, k_sw, zero).astype(BF16)]
    k_hi = [jnp.where(low, zero, k_sw).astype(BF16), jnp.where(low, zero, k_ext).astype(BF16)]
    v_lo = [jnp.where(low, v_ext, zero).astype(BF16), jnp.where(low, v_sw, zero).astype(BF16)]
    v_hi = [jnp.where(low, zero, v_sw).astype(BF16), jnp.where(low, zero, v_ext).astype(BF16)]

    n_blocks = ts // ATTN_BLOCK
    pair = 2 * ATTN_BLOCK
    qi = lax.broadcasted_iota(jnp.int32, (pair, pair), 0) % ATTN_BLOCK
    kj = lax.broadcasted_iota(jnp.int32, (pair, pair), 1)
    band = (kj > qi) & (kj <= qi + ATTN_BLOCK)
    band_first = band & (kj >= jnp.where(ti > 0, 0, ATTN_BLOCK))
    upper_rows = lax.broadcasted_iota(jnp.int32, (pair, 1), 0) >= ATTN_BLOCK

    q_bf = q.astype(BF16)
    units = [(jb, g) for jb in range(n_blocks) for g in range(ATTN_KV_HEADS)]

    def scores(jb, g):
        r0 = jb * ATTN_BLOCK
        qs = jnp.concatenate([q_bf[r0:r0 + ATTN_BLOCK, (2 * g + i) * LANES:(2 * g + i + 1) * LANES]
                              for i in range(2)], axis=0)
        kk = jnp.concatenate([k_lo[g][r0:r0 + pair], k_hi[g][r0:r0 + pair]], axis=0)
        return _dot_nt(qs, kk)

    c0 = ATTN_WIDTH + 2 * KV_WIDTH
    conv_starts = [c0 + i * 2 * LANES for i in range(3 * CONV_WIDTH // (2 * LANES))]
    conv_parts = []
    s_next = scores(*units[0])
    for ui, (jb, g) in enumerate(units):
        r0 = jb * ATTN_BLOCK
        s_cur = s_next
        if ui + 1 < len(units):
            s_next = scores(*units[ui + 1])
        if conv_starts:
            cs = conv_starts.pop(0)
            conv_parts.append(_dot(h, w_in_ref[:, cs:cs + 2 * LANES]))
        mask = band_first if jb == 0 else band
        probs, invs = [], []
        for half in range(2):
            s = jnp.where(mask, s_cur[:, half * pair:(half + 1) * pair], NEG_BIG)
            sk = jnp.where(upper_rows, sink_ref[4 * g + 2 + half], sink_ref[4 * g + half]) * LOG2_E
            m = jnp.maximum(jnp.max(s, axis=-1, keepdims=True), sk)
            p = jnp.exp2(s - m)
            invs.append(1.0 / (jnp.sum(p, axis=-1, keepdims=True) + jnp.exp2(sk - m)))
            probs.append(p.astype(BF16))
        vv = jnp.concatenate([v_lo[g][r0:r0 + pair], v_hi[g][r0:r0 + pair]], axis=0)
        o = _dot(jnp.concatenate(probs, axis=1), vv)
        o = o * jnp.where(low, invs[0], invs[1])
        for i in range(2):
            attn_buf[r0:r0 + ATTN_BLOCK, (2 * g + i) * LANES:(2 * g + i + 1) * LANES] = (
                o[i * ATTN_BLOCK:(i + 1) * ATTN_BLOCK].astype(BF16))
    cb = jnp.concatenate(conv_parts[0:2], axis=1)
    cc = jnp.concatenate(conv_parts[2:4], axis=1)
    cx = jnp.concatenate(conv_parts[4:6], axis=1)

    y = cc * cx
    tail = y_tail[...]
    conv = (conv_w_ref[2:3, :] * y + conv_w_ref[1:2, :] * _shift_rows(y, tail, 1)
            + conv_w_ref[0:1, :] * _shift_rows(y, tail, 2))
    y_tail[...] = y[ts - SUBLANES:]
    gated = (cb * conv).astype(BF16)

    out = _dot(attn_buf[...], w_out_ref[0:ATTN_WIDTH, :]) + _dot(gated, w_out_ref[ATTN_WIDTH:, :])
    o_ref[0] = x + out


def _hy_mixer(x, positions, norm, w_in, q_norm, k_norm, sinks, conv_w, w_out):
    b, s, d = x.shape
    ts = SEQ_TILE
    inv_freq = ROPE_THETA ** (-jnp.arange(0, HEAD_DIM, 2, dtype=F32) / HEAD_DIM)
    invf = jnp.broadcast_to(jnp.tile(inv_freq, LANES // (HEAD_DIM // 2))[:, None], (LANES, LANES))
    head_id = jnp.arange(ATTN_WIDTH) // HEAD_DIM
    hsum = jnp.where(head_id[:, None] == head_id[None, :], 1.0 / HEAD_DIM, 0.0).astype(BF16)
    qn = jnp.tile(q_norm, ATTN_Q_HEADS)[None, :]
    kn = jnp.tile(k_norm, ATTN_KV_HEADS)[None, :]
    const = lambda *shape: pl.BlockSpec(shape, lambda bi, ti: (0,) * len(shape),
                                        pipeline_mode=pl.Buffered(1))
    return pl.pallas_call(
        _hy_mixer_kernel,
        grid=(b, s // ts),
        in_specs=[
            pl.BlockSpec((1, ts, d), lambda bi, ti: (bi, ti, 0)),
            pl.BlockSpec((1, 1, ts), lambda bi, ti: (bi, 0, ti)),
            const(LANES, LANES),
            const(1, d),
            const(d, HY_IN_WIDTH),
            const(1, ATTN_WIDTH),
            const(1, KV_WIDTH),
            pl.BlockSpec(memory_space=pltpu.SMEM),
            const(ATTN_WIDTH, ATTN_WIDTH),
            const(3, CONV_WIDTH),
            const(d, d),
        ],
        out_specs=pl.BlockSpec((1, ts, d), lambda bi, ti: (bi, ti, 0)),
        out_shape=jax.ShapeDtypeStruct(x.shape, x.dtype),
        scratch_shapes=[
            pltpu.VMEM((ATTN_BLOCK, KV_WIDTH), F32),
            pltpu.VMEM((ATTN_BLOCK, KV_WIDTH), F32),
            pltpu.VMEM((SUBLANES, CONV_WIDTH), F32),
            pltpu.VMEM((ts, ATTN_WIDTH), BF16),
        ],
        compiler_params=pltpu.CompilerParams(
            dimension_semantics=("arbitrary", "arbitrary"), vmem_limit_bytes=VMEM_LIMIT),
        name="hy_mixer",
    )(x, positions.reshape(b, 1, s), invf, norm[None, :], w_in.astype(BF16), qn, kn, sinks, hsum,
      conv_w, w_out.astype(BF16))


def _gla_mixer_kernel(x_ref, norm_ref, w_qkv_ref, w_glr_ref, w_og_ref, w_gup_ref, gbias_ref, tril_ref,
                      onorm_ref, w_out_ref, o_ref, state, u_buf, sc_buf, o_buf):
    ti = pl.program_id(1)
    ts = x_ref.shape[1]
    c = GLA_CHUNK
    nc = ts // c
    nsub = c // GLA_SUB
    mid = GLA_SUB // 2
    kd = GLA_KEY_DIM

    @pl.when(ti == 0)
    def _():
        state[...] = jnp.zeros_like(state)

    x = x_ref[0]
    h = _rms_norm(x, norm_ref[...]).astype(BF16)

    g_lr = _dot(h, w_glr_ref[...])
    q_all = _dot(h, w_qkv_ref[:, 0:kd]) * (GLA_DK ** -0.5)
    z = _dot(g_lr.astype(BF16), w_gup_ref[...]) + gbias_ref[...]
    k_all = _dot(h, w_qkv_ref[:, kd:2 * kd])
    og = _dot(h, w_og_ref[...])
    log_a = (jnp.minimum(z, 0.0) - jnp.log(1.0 + jnp.exp(-jnp.abs(z)))) * (LOG2_E / GLA_GATE_NORMALIZER)
    la_hi = log_a.astype(BF16)
    la_lo = (log_a - la_hi.astype(F32)).astype(BF16)
    tril = tril_ref[...]
    tb = tril.shape[0]
    gc = jnp.concatenate([_dot(tril, la_hi[r:r + tb]) + _dot(tril, la_lo[r:r + tb]) for r in range(0, ts, tb)],
                         axis=0)
    v_all = _dot(h, w_qkv_ref[:, 2 * kd:]).astype(BF16)
    gate = og * jax.nn.sigmoid(og)

    g3 = gc.reshape(nc, c, kd)
    q3 = q_all.reshape(nc, c, kd)
    k3 = k_all.reshape(nc, c, kd)
    g_end = g3[:, c - 1:c, :]
    qe = (q3 * jnp.exp2(g3)).astype(BF16)
    k_dec = (k3 * jnp.exp2(g_end - g3)).astype(BF16)
    e_end = jnp.exp2(g_end)
    g_refs = [g3[:, sb * GLA_SUB + mid:sb * GLA_SUB + mid + 1, :] for sb in range(nsub)]
    g_own = jnp.concatenate([jnp.broadcast_to(r, (nc, GLA_SUB, kd)) for r in g_refs], axis=1)
    qt = (q3 * jnp.exp2(g3 - g_own)).astype(BF16)
    kts = []
    for sb in range(nsub):
        n = (sb + 1) * GLA_SUB
        kts.append((k3[:, 0:n, :] * jnp.exp2(g_refs[sb] - g3[:, 0:n, :])).astype(BF16))

    ri = lax.broadcasted_iota(jnp.int32, (c, c), 0)
    cj = lax.broadcasted_iota(jnp.int32, (c, c), 1)
    causal = cj <= ri
    zq = jnp.zeros((GLA_SUB, GLA_DK), BF16)

    e_cols = e_end.reshape(nc, kd).T

    def local_dots(ci):
        r0 = ci * c
        for hd in range(GLA_HEADS):
            ks = slice(hd * GLA_DK, (hd + 1) * GLA_DK)
            vh = v_all[r0:r0 + c, hd * GLA_DV:(hd + 1) * GLA_DV]
            q_cols, k_cols = [], []
            for sb in range(nsub):
                s0 = sb * GLA_SUB
                n = s0 + GLA_SUB
                q_cols.append(jnp.concatenate([zq] * sb + [qt[ci, s0:n, ks]] + [zq] * (nsub - 1 - sb), axis=0))
                k_cols.append(jnp.concatenate([kts[sb][ci, :, ks]] + [zq] * (nsub - 1 - sb), axis=0))
            sc = _dot_nt(jnp.concatenate(q_cols, axis=1), jnp.concatenate(k_cols, axis=1))
            sc_buf[ci, hd] = jnp.where(causal, sc, 0.0).astype(BF16)
            u_buf[ci, hd] = _dot_tn(k_dec[ci, :, ks], vh)

    sts = [state[hd] for hd in range(GLA_HEADS)]
    local_dots(0)
    for ci in range(nc):
        r0 = ci * c
        if ci + 1 < nc:
            local_dots(ci + 1)
        for hd in range(GLA_HEADS):
            ks = slice(hd * GLA_DK, (hd + 1) * GLA_DK)
            vh = v_all[r0:r0 + c, hd * GLA_DV:(hd + 1) * GLA_DV]
            lhs = jnp.concatenate([qe[ci, :, ks], sc_buf[ci, hd]], axis=1)
            rhs = jnp.concatenate([sts[hd].astype(BF16), vh], axis=0)
            o_buf[r0:r0 + c, hd * GLA_DV:(hd + 1) * GLA_DV] = _dot(lhs, rhs)
            sts[hd] = e_cols[hd * GLA_DK:(hd + 1) * GLA_DK, ci:ci + 1] * sts[hd] + u_buf[ci, hd]
    for hd in range(GLA_HEADS):
        state[hd] = sts[hd]
    parts = []
    for hd in range(GLA_HEADS):
        vs = slice(hd * GLA_DV, (hd + 1) * GLA_DV)
        o = o_buf[:, vs]
        o = o * lax.rsqrt(jnp.mean(o * o, axis=-1, keepdims=True) + RMS_EPS) * onorm_ref[...]
        parts.append((o * gate[:, vs]).astype(BF16))
    o_ref[0] = x + _dot(jnp.concatenate(parts, axis=1), w_out_ref[...])


def _gla_mixer(x, norm, w_in, w_gate_up, gate_bias, o_norm, w_out):
    b, s, d = x.shape
    ts = SEQ_TILE
    c_qkv = 2 * GLA_KEY_DIM + GLA_VAL_DIM
    w_qkv = w_in[:, :c_qkv].astype(BF16)
    w_glr = jnp.pad(w_in[:, c_qkv:c_qkv + GLA_GATE_RANK], ((0, 0), (0, LANES - GLA_GATE_RANK))).astype(BF16)
    w_og = w_in[:, c_qkv + GLA_GATE_RANK:].astype(BF16)
    w_gup = jnp.pad(w_gate_up, ((0, LANES - GLA_GATE_RANK), (0, 0))).astype(BF16)
    rows = jnp.arange(GLA_TRIL_ROWS)
    tril = ((rows[:, None] >= rows[None, :]) & (rows[:, None] // GLA_CHUNK == rows[None, :] // GLA_CHUNK)).astype(BF16)
    const = lambda *shape: pl.BlockSpec(shape, lambda bi, ti: (0,) * len(shape),
                                        pipeline_mode=pl.Buffered(1))
    return pl.pallas_call(
        _gla_mixer_kernel,
        grid=(b, s // ts),
        in_specs=[
            pl.BlockSpec((1, ts, d), lambda bi, ti: (bi, ti, 0)),
            const(1, d),
            const(d, c_qkv),
            const(d, LANES),
            const(d, GLA_VAL_DIM),
            const(LANES, GLA_KEY_DIM),
            const(1, GLA_KEY_DIM),
            const(GLA_TRIL_ROWS, GLA_TRIL_ROWS),
            const(1, GLA_DV),
            const(GLA_VAL_DIM, d),
        ],
        out_specs=pl.BlockSpec((1, ts, d), lambda bi, ti: (bi, ti, 0)),
        out_shape=jax.ShapeDtypeStruct(x.shape, x.dtype),
        scratch_shapes=[
            pltpu.VMEM((GLA_HEADS, GLA_DK, GLA_DV), F32),
            pltpu.VMEM((ts // GLA_CHUNK, GLA_HEADS, GLA_DK, GLA_DV), F32),
            pltpu.VMEM((ts // GLA_CHUNK, GLA_HEADS, GLA_CHUNK, GLA_CHUNK), BF16),
            pltpu.VMEM((ts, GLA_VAL_DIM), F32),
        ],
        compiler_params=pltpu.CompilerParams(
            dimension_semantics=("arbitrary", "arbitrary"), vmem_limit_bytes=VMEM_LIMIT),
        name="gla_mixer",
    )(x, norm[None, :], w_qkv, w_glr, w_og, w_gup, gate_bias[None, :], tril, o_norm[None, :],
      w_out.astype(BF16))


def _ffn_ple_kernel(x_ref, p_ref, fnorm_ref, w_up_ref, conv_w_ref, conv_b_ref, w_down_ref, pnorm_ref,
                    w_gate_ref, w_proj_ref, o_ref, s0_tail, s1_tail, act_buf):
    ti = pl.program_id(1)
    ts = x_ref.shape[1]

    @pl.when(ti == 0)
    def _():
        s0_tail[...] = jnp.zeros_like(s0_tail)
        s1_tail[...] = jnp.zeros_like(s1_tail)

    rb = FFN_ROW_BLOCK
    n_chunks = D_FF // FF_CHUNK

    def conv_cols(u, c0):
        cols = slice(c0, c0 + FF_CHUNK)
        s0 = conv_w_ref[0:1, cols] * u
        s1 = conv_w_ref[1:2, cols] * u + _shift_rows(s0, s0_tail[:, cols], 1)
        out = conv_w_ref[2:3, cols] * u + _shift_rows(s1, s1_tail[:, cols], 1) + conv_b_ref[:, cols]
        s0_tail[:, cols] = s0[rb - SUBLANES:]
        s1_tail[:, cols] = s1[rb - SUBLANES:]
        return out

    def up_dots(h, ci):
        return (_dot(h, w_up_ref[:, ci * FF_CHUNK:(ci + 1) * FF_CHUNK]),
                _dot(h, w_up_ref[:, D_FF + ci * FF_CHUNK:D_FF + (ci + 1) * FF_CHUNK]))

    def embed(r0, x1):
        hp = _rms_norm(x1, pnorm_ref[...]).astype(BF16)
        gate = jax.nn.sigmoid(_dot(hp, w_gate_ref[...]))
        emb = _dot(p_ref[0, r0:r0 + rb, :].astype(BF16), w_proj_ref[...])
        o_ref[0, r0:r0 + rb, :] = x1 + gate * emb

    prev = None
    for r0 in range(0, ts, rb):
        h = _rms_norm(x_ref[0, r0:r0 + rb, :], fnorm_ref[...]).astype(BF16)
        nxt = up_dots(h, 0)
        acc = None
        for ci in range(n_chunks):
            cols = slice(ci * FF_CHUNK, (ci + 1) * FF_CHUNK)
            u_gate, u_up = nxt
            if ci + 1 < n_chunks:
                nxt = up_dots(h, ci + 1)
            if prev is not None:
                d = _dot(act_buf[prev:prev + rb, cols], w_down_ref[cols, :])
                acc = d if acc is None else acc + d
            gate = conv_cols(u_gate, ci * FF_CHUNK)
            up = conv_cols(u_up, D_FF + ci * FF_CHUNK)
            act_buf[r0:r0 + rb, cols] = (gate * (1.0 + lax.erf(gate)) * up).astype(BF16)
        if prev is not None:
            embed(prev, x_ref[0, prev:prev + rb, :] + acc)
        prev = r0
    embed(prev, x_ref[0, prev:prev + rb, :] + _dot(act_buf[prev:prev + rb, :], w_down_ref[...]))


def _ffn_ple(x, p, layer, fnorm, w_up, conv_w, conv_b, w_down, pnorm, w_gate, w_proj):
    b, s, d = x.shape
    ts = FFN_SEQ_TILE
    col_scale = jnp.where(jnp.arange(2 * D_FF) < D_FF, 2.0 ** -0.5, 1.0).astype(F32)[None, :]
    const = lambda *shape: pl.BlockSpec(shape, lambda bi, ti: (0,) * len(shape),
                                        pipeline_mode=pl.Buffered(1))
    return pl.pallas_call(
        _ffn_ple_kernel,
        grid=(b, s // ts),
        in_specs=[
            pl.BlockSpec((1, ts, d), lambda bi, ti: (bi, ti, 0)),
            pl.BlockSpec((None, 1, ts, PLE_DIM), lambda bi, ti: (layer, bi, ti, 0)),
            const(1, d),
            const(d, 2 * D_FF),
            const(3, 2 * D_FF),
            const(1, 2 * D_FF),
            const(D_FF, d),
            const(1, d),
            const(d, d),
            const(PLE_DIM, d),
        ],
        out_specs=pl.BlockSpec((1, ts, d), lambda bi, ti: (bi, ti, 0)),
        out_shape=jax.ShapeDtypeStruct(x.shape, x.dtype),
        scratch_shapes=[pltpu.VMEM((SUBLANES, 2 * D_FF), F32), pltpu.VMEM((SUBLANES, 2 * D_FF), F32),
                        pltpu.VMEM((ts, D_FF), BF16)],
        compiler_params=pltpu.CompilerParams(
            dimension_semantics=("arbitrary", "arbitrary"), vmem_limit_bytes=VMEM_LIMIT),
        name="ffn_ple",
    )(x, p, fnorm[None, :], (w_up * col_scale).astype(BF16), conv_w, (conv_b * col_scale[0])[None, :],
      (w_down * (0.5 * 2.0 ** 0.5)).astype(BF16),
      pnorm[None, :], w_gate.astype(BF16), w_proj.astype(BF16))


def kernel(x, p, positions, mix_norm, ffn_norm, ffn_w_up, ffn_conv_w, ffn_conv_b, ffn_w_down, ple_norm,
           ple_w_gate, ple_w_proj, hy_w_in, hy_q_norm, hy_k_norm, hy_sinks, hy_conv_w, hy_w_out, gla_w_in,
           gla_w_gate_up, gla_gate_bias, gla_o_norm, gla_w_out):
    depth = mix_norm.shape[0]
    for i in range(depth):
        j = i // 2
        if i % 2 == 0:
            x = _hy_mixer(x, positions, mix_norm[i], hy_w_in[j], hy_q_norm[j], hy_k_norm[j], hy_sinks[j],
                          hy_conv_w[j], hy_w_out[j])
        else:
            x = _gla_mixer(x, mix_norm[i], gla_w_in[j], gla_w_gate_up[j], gla_gate_bias[j], gla_o_norm[j],
                           gla_w_out[j])
        x = _ffn_ple(x, p, i, ffn_norm[i], ffn_w_up[i], ffn_conv_w[i], ffn_conv_b[i], ffn_w_down[i],
                     ple_norm[i], ple_w_gate[i], ple_w_proj[i])
    return x
```

```python
import jax
import jax.numpy as jnp
from jax import lax
from jax.experimental import pallas as pl
from jax.experimental.pallas import tpu as pltpu

D_MODEL = 1024
HEAD_DIM = 64
ATTN_Q_HEADS = 8
ATTN_KV_HEADS = 2
ATTN_BLOCK = 128
ROPE_THETA = 10000.0
ATTN_WIDTH = ATTN_Q_HEADS * HEAD_DIM
KV_WIDTH = ATTN_KV_HEADS * HEAD_DIM
CONV_WIDTH = D_MODEL // 2
HY_IN_WIDTH = ATTN_WIDTH + 2 * KV_WIDTH + 3 * CONV_WIDTH
GLA_HEADS = 4
GLA_KEY_DIM = D_MODEL // 2
GLA_VAL_DIM = D_MODEL
GLA_DK = GLA_KEY_DIM // GLA_HEADS
GLA_DV = GLA_VAL_DIM // GLA_HEADS
GLA_GATE_RANK = 16
GLA_GATE_NORMALIZER = 16.0
GLA_CHUNK = 64
GLA_TRIL_ROWS = 256
GLA_SUB = 16
D_FF = 2816
PLE_DIM = 256
RMS_EPS = 1e-6

LANES = 128
SUBLANES = 8
SEQ_TILE = 1024
FFN_SEQ_TILE = 512
FFN_ROW_BLOCK = 256
FF_CHUNK = 256
VMEM_LIMIT = 56 * 1024 * 1024
NEG_BIG = -1e30
LOG2_E = 1.4426950408889634

F32 = jnp.float32
BF16 = jnp.bfloat16


def _dot(a, b):
    return jnp.dot(a, b, preferred_element_type=F32)


def _dot_nt(a, b):
    return lax.dot_general(a, b, (((1,), (1,)), ((), ())), preferred_element_type=F32)


def _dot_tn(a, b):
    return lax.dot_general(a, b, (((0,), (0,)), ((), ())), preferred_element_type=F32)


def _rms_norm(x, gain):
    return x * lax.rsqrt(jnp.mean(x * x, axis=-1, keepdims=True) + RMS_EPS) * gain


def _shift_rows(u, tail, k):
    n = u.shape[0]
    ext = jnp.concatenate([tail, u], axis=0)
    return ext[SUBLANES - k:SUBLANES - k + n]


def _hy_mixer_kernel(x_ref, pos_ref, invf_ref, norm_ref, w_in_ref, qn_ref, kn_ref, sink_ref, hsum_ref,
                     conv_w_ref, w_out_ref, o_ref, k_tail, v_tail, y_tail, attn_buf):
    ti = pl.program_id(1)
    ts = x_ref.shape[1]

    @pl.when(ti == 0)
    def _():
        k_tail[...] = jnp.zeros_like(k_tail)
        v_tail[...] = jnp.zeros_like(v_tail)
        y_tail[...] = jnp.zeros_like(y_tail)

    x = x_ref[0]
    h = _rms_norm(x, norm_ref[...]).astype(BF16)

    pos = pos_ref[0].astype(F32)
    n_freq = HEAD_DIM // 2
    ang = invf_ref[0:n_freq, 0:1] * pos
    cos_t = jnp.concatenate([jnp.cos(ang)] * (LANES // n_freq), axis=0).T
    sin_t = jnp.concatenate([jnp.sin(ang)] * (LANES // n_freq), axis=0).T
    lane = lax.broadcasted_iota(jnp.int32, (1, LANES), 1)
    first_half = (lane % HEAD_DIM) < (HEAD_DIM // 2)
    sin_t = jnp.where(first_half, -sin_t, sin_t)

    def qk_norm_rope(t, gain_ref):
        w = t.shape[1]
        msq = _dot((t * t).astype(BF16), hsum_ref[0:w, 0:w])
        t = t * lax.rsqrt(msq + RMS_EPS) * gain_ref[...]
        reps = w // LANES
        cos_w = jnp.concatenate([cos_t] * reps, axis=1) if reps > 1 else cos_t
        sin_w = jnp.concatenate([sin_t] * reps, axis=1) if reps > 1 else sin_t
        fh = jnp.concatenate([first_half] * reps, axis=1) if reps > 1 else first_half
        half = HEAD_DIM // 2
        rot = jnp.where(fh, pltpu.roll(t, w - half, axis=1), pltpu.roll(t, half, axis=1))
        return t * cos_w + rot * sin_w

    q = _dot(h, w_in_ref[:, 0:ATTN_WIDTH])
    q = qk_norm_rope(q, qn_ref) * (HEAD_DIM ** -0.5 * LOG2_E)
    k = _dot(h, w_in_ref[:, ATTN_WIDTH:ATTN_WIDTH + KV_WIDTH])
    k = qk_norm_rope(k, kn_ref)
    v = _dot(h, w_in_ref[:, ATTN_WIDTH + KV_WIDTH:ATTN_WIDTH + 2 * KV_WIDTH])

    k_ext = jnp.concatenate([k_tail[...], k], axis=0)
    v_ext = jnp.concatenate([v_tail[...], v], axis=0)
    k_tail[...] = k[ts - ATTN_BLOCK:]
    v_tail[...] = v[ts - ATTN_BLOCK:]

    low = lane < HEAD_DIM
    k_sw = pltpu.roll(k_ext, HEAD_DIM, axis=1)
    v_sw = pltpu.roll(v_ext, HEAD_DIM, axis=1)
    zero = jnp.zeros_like(k_ext)
    k_lo = [jnp.where(low, k_ext, zero).astype(BF16), jnp.where(low, k_sw, zero).astype(BF16)]
    k_hi = [jnp.where(low, zero, k_sw).astype(BF16), jnp.where(low, zero, k_ext).astype(BF16)]
    v_lo = [jnp.where(low, v_ext, zero).astype(BF16), jnp.where(low, v_sw, zero).astype(BF16)]
    v_hi = [jnp.where(low, zero, v_sw).astype(BF16), jnp.where(low, zero, v_ext).astype(BF16)]

    n_blocks = ts // ATTN_BLOCK
    pair = 2 * ATTN_BLOCK
    qi = lax.broadcasted_iota(jnp.int32, (pair, pair), 0) % ATTN_BLOCK
    kj = lax.broadcasted_iota(jnp.int32, (pair, pair), 1)
    band = (kj > qi) & (kj <= qi + ATTN_BLOCK)
    band_first = band & (kj >= jnp.where(ti > 0, 0, ATTN_BLOCK))
    upper_rows = lax.broadcasted_iota(jnp.int32, (pair, 1), 0) >= ATTN_BLOCK

    q_bf = q.astype(BF16)
    units = [(jb, g) for jb in range(n_blocks) for g in range(ATTN_KV_HEADS)]

    def scores(jb, g):
        r0 = jb * ATTN_BLOCK
        qs = jnp.concatenate([q_bf[r0:r0 + ATTN_BLOCK, (2 * g + i) * LANES:(2 * g + i + 1) * LANES]
                              for i in range(2)], axis=0)
        kk = jnp.concatenate([k_lo[g][r0:r0 + pair], k_hi[g][r0:r0 + pair]], axis=0)
        return _dot_nt(qs, kk)

    c0 = ATTN_WIDTH + 2 * KV_WIDTH
    conv_starts = [c0 + i * 2 * LANES for i in range(3 * CONV_WIDTH // (2 * LANES))]
    conv_parts = []
    s_next = scores(*units[0])
    for ui, (jb, g) in enumerate(units):
        r0 = jb * ATTN_BLOCK
        s_cur = s_next
        if ui + 1 < len(units):
            s_next = scores(*units[ui + 1])
        if conv_starts:
            cs = conv_starts.pop(0)
            conv_parts.append(_dot(h, w_in_ref[:, cs:cs + 2 * LANES]))
        mask = band_first if jb == 0 else band
        probs, invs = [], []
        for half in range(2):
            s = jnp.where(mask, s_cur[:, half * pair:(half + 1) * pair], NEG_BIG)
            sk = jnp.where(upper_rows, sink_ref[4 * g + 2 + half], sink_ref[4 * g + half]) * LOG2_E
            m = jnp.maximum(jnp.max(s, axis=-1, keepdims=True), sk)
            p = jnp.exp2(s - m)
            invs.append(1.0 / (jnp.sum(p, axis=-1, keepdims=True) + jnp.exp2(sk - m)))
            probs.append(p.astype(BF16))
        vv = jnp.concatenate([v_lo[g][r0:r0 + pair], v_hi[g][r0:r0 + pair]], axis=0)
        o = _dot(jnp.concatenate(probs, axis=1), vv)
        o = o * jnp.where(low, invs[0], invs[1])
        for i in range(2):
            attn_buf[r0:r0 + ATTN_BLOCK, (2 * g + i) * LANES:(2 * g + i + 1) * LANES] = (
                o[i * ATTN_BLOCK:(i + 1) * ATTN_BLOCK].astype(BF16))
    cb = jnp.concatenate(conv_parts[0:2], axis=1)
    cc = jnp.concatenate(conv_parts[2:4], axis=1)
    cx = jnp.concatenate(conv_parts[4:6], axis=1)

    y = cc * cx
    tail = y_tail[...]
    conv = (conv_w_ref[2:3, :] * y + conv_w_ref[1:2, :] * _shift_rows(y, tail, 1)
            + conv_w_ref[0:1, :] * _shift_rows(y, tail, 2))
    y_tail[...] = y[ts - SUBLANES:]
    gated = (cb * conv).astype(BF16)

    out = _dot(attn_buf[...], w_out_ref[0:ATTN_WIDTH, :]) + _dot(gated, w_out_ref[ATTN_WIDTH:, :])
    o_ref[0] = x + out


def _hy_mixer(x, positions, norm, w_in, q_norm, k_norm, sinks, conv_w, w_out):
    b, s, d = x.shape
    ts = SEQ_TILE
    inv_freq = ROPE_THETA ** (-jnp.arange(0, HEAD_DIM, 2, dtype=F32) / HEAD_DIM)
    invf = jnp.broadcast_to(jnp.tile(inv_freq, LANES // (HEAD_DIM // 2))[:, None], (LANES, LANES))
    head_id = jnp.arange(ATTN_WIDTH) // HEAD_DIM
    hsum = jnp.where(head_id[:, None] == head_id[None, :], 1.0 / HEAD_DIM, 0.0).astype(BF16)
    qn = jnp.tile(q_norm, ATTN_Q_HEADS)[None, :]
    kn = jnp.tile(k_norm, ATTN_KV_HEADS)[None, :]
    const = lambda *shape: pl.BlockSpec(shape, lambda bi, ti: (0,) * len(shape),
                                        pipeline_mode=pl.Buffered(1))
    return pl.pallas_call(
        _hy_mixer_kernel,
        grid=(b, s // ts),
        in_specs=[
            pl.BlockSpec((1, ts, d), lambda bi, ti: (bi, ti, 0)),
            pl.BlockSpec((1, 1, ts), lambda bi, ti: (bi, 0, ti)),
            const(LANES, LANES),
            const(1, d),
            const(d, HY_IN_WIDTH),
            const(1, ATTN_WIDTH),
            const(1, KV_WIDTH),
            pl.BlockSpec(memory_space=pltpu.SMEM),
            const(ATTN_WIDTH, ATTN_WIDTH),
            const(3, CONV_WIDTH),
            const(d, d),
        ],
        out_specs=pl.BlockSpec((1, ts, d), lambda bi, ti: (bi, ti, 0)),
        out_shape=jax.ShapeDtypeStruct(x.shape, x.dtype),
        scratch_shapes=[
            pltpu.VMEM((ATTN_BLOCK, KV_WIDTH), F32),
            pltpu.VMEM((ATTN_BLOCK, KV_WIDTH), F32),
            pltpu.VMEM((SUBLANES, CONV_WIDTH), F32),
            pltpu.VMEM((ts, ATTN_WIDTH), BF16),
        ],
        compiler_params=pltpu.CompilerParams(
            dimension_semantics=("arbitrary", "arbitrary"), vmem_limit_bytes=VMEM_LIMIT),
        name="hy_mixer",
    )(x, positions.reshape(b, 1, s), invf, norm[None, :], w_in.astype(BF16), qn, kn, sinks, hsum,
      conv_w, w_out.astype(BF16))


def _gla_mixer_kernel(x_ref, norm_ref, w_qkv_ref, w_glr_ref, w_og_ref, w_gup_ref, gbias_ref, tril_ref,
                      onorm_ref, w_out_ref, o_ref, state, u_buf, sc_buf, o_buf):
    ti = pl.program_id(1)
    ts = x_ref.shape[1]
    c = GLA_CHUNK
    nc = ts // c
    nsub = c // GLA_SUB
    mid = GLA_SUB // 2
    kd = GLA_KEY_DIM

    @pl.when(ti == 0)
    def _():
        state[...] = jnp.zeros_like(state)

    x = x_ref[0]
    h = _rms_norm(x, norm_ref[...]).astype(BF16)

    g_lr = _dot(h, w_glr_ref[...])
    q_all = _dot(h, w_qkv_ref[:, 0:kd]) * (GLA_DK ** -0.5)
    z = _dot(g_lr.astype(BF16), w_gup_ref[...]) + gbias_ref[...]
    k_all = _dot(h, w_qkv_ref[:, kd:2 * kd])
    og = _dot(h, w_og_ref[...])
    log_a = (jnp.minimum(z, 0.0) - jnp.log(1.0 + jnp.exp(-jnp.abs(z)))) * (LOG2_E / GLA_GATE_NORMALIZER)
    la_hi = log_a.astype(BF16)
    la_lo = (log_a - la_hi.astype(F32)).astype(BF16)
    tril = tril_ref[...]
    tb = tril.shape[0]
    gc = jnp.concatenate([_dot(tril, la_hi[r:r + tb]) + _dot(tril, la_lo[r:r + tb]) for r in range(0, ts, tb)],
                         axis=0)
    v_all = _dot(h, w_qkv_ref[:, 2 * kd:]).astype(BF16)
    gate = og * jax.nn.sigmoid(og)

    g3 = gc.reshape(nc, c, kd)
    q3 = q_all.reshape(nc, c, kd)
    k3 = k_all.reshape(nc, c, kd)
    g_end = g3[:, c - 1:c, :]
    qe = (q3 * jnp.exp2(g3)).astype(BF16)
    k_dec = (k3 * jnp.exp2(g_end - g3)).astype(BF16)
    e_end = jnp.exp2(g_end)
    g_refs = [g3[:, sb * GLA_SUB + mid:sb * GLA_SUB + mid + 1, :] for sb in range(nsub)]
    g_own = jnp.concatenate([jnp.broadcast_to(r, (nc, GLA_SUB, kd)) for r in g_refs], axis=1)
    qt = (q3 * jnp.exp2(g3 - g_own)).astype(BF16)
    kts = []
    for sb in range(nsub):
        n = (sb + 1) * GLA_SUB
        kts.append((k3[:, 0:n, :] * jnp.exp2(g_refs[sb] - g3[:, 0:n, :])).astype(BF16))

    ri = lax.broadcasted_iota(jnp.int32, (c, c), 0)
    cj = lax.broadcasted_iota(jnp.int32, (c, c), 1)
    causal = cj <= ri
    zq = jnp.zeros((GLA_SUB, GLA_DK), BF16)

    e_cols = e_end.reshape(nc, kd).T

    def local_dots(ci):
        r0 = ci * c
        for hd in range(GLA_HEADS):
            ks = slice(hd * GLA_DK, (hd + 1) * GLA_DK)
            vh = v_all[r0:r0 + c, hd * GLA_DV:(hd + 1) * GLA_DV]
            q_cols, k_cols = [], []
            for sb in range(nsub):
                s0 = sb * GLA_SUB
                n = s0 + GLA_SUB
                q_cols.append(jnp.concatenate([zq] * sb + [qt[ci, s0:n, ks]] + [zq] * (nsub - 1 - sb), axis=0))
                k_cols.append(jnp.concatenate([kts[sb][ci, :, ks]] + [zq] * (nsub - 1 - sb), axis=0))
            sc = _dot_nt(jnp.concatenate(q_cols, axis=1), jnp.concatenate(k_cols, axis=1))
            sc_buf[ci, hd] = jnp.where(causal, sc, 0.0).astype(BF16)
            u_buf[ci, hd] = _dot_tn(k_dec[ci, :, ks], vh)

    sts = [state[hd] for hd in range(GLA_HEADS)]
    local_dots(0)
    for ci in range(nc):
        r0 = ci * c
        if ci + 1 < nc:
            local_dots(ci + 1)
        for hd in range(GLA_HEADS):
            ks = slice(hd * GLA_DK, (hd + 1) * GLA_DK)
            vh = v_all[r0:r0 + c, hd * GLA_DV:(hd + 1) * GLA_DV]
            lhs = jnp.concatenate([qe[ci, :, ks], sc_buf[ci, hd]], axis=1)
            rhs = jnp.concatenate([sts[hd].astype(BF16), vh], axis=0)
            o_buf[r0:r0 + c, hd * GLA_DV:(hd + 1) * GLA_DV] = _dot(lhs, rhs)
            sts[hd] = e_cols[hd * GLA_DK:(hd + 1) * GLA_DK, ci:ci + 1] * sts[hd] + u_buf[ci, hd]
    for hd in range(GLA_HEADS):
        state[hd] = sts[hd]
    parts = []
    for hd in range(GLA_HEADS):
        vs = slice(hd * GLA_DV, (hd + 1) * GLA_DV)
        o = o_buf[:, vs]
        o = o * lax.rsqrt(jnp.mean(o * o, axis=-1, keepdims=True) + RMS_EPS) * onorm_ref[...]
        parts.append((o * gate[:, vs]).astype(BF16))
    o_ref[0] = x + _dot(jnp.concatenate(parts, axis=1), w_out_ref[...])


def _gla_mixer(x, norm, w_in, w_gate_up, gate_bias, o_norm, w_out):
    b, s, d = x.shape
    ts = SEQ_TILE
    c_qkv = 2 * GLA_KEY_DIM + GLA_VAL_DIM
    w_qkv = w_in[:, :c_qkv].astype(BF16)
    w_glr = jnp.pad(w_in[:, c_qkv:c_qkv + GLA_GATE_RANK], ((0, 0), (0, LANES - GLA_GATE_RANK))).astype(BF16)
    w_og = w_in[:, c_qkv + GLA_GATE_RANK:].astype(BF16)
    w_gup = jnp.pad(w_gate_up, ((0, LANES - GLA_GATE_RANK), (0, 0))).astype(BF16)
    rows = jnp.arange(GLA_TRIL_ROWS)
    tril = ((rows[:, None] >= rows[None, :]) & (rows[:, None] // GLA_CHUNK == rows[None, :] // GLA_CHUNK)).astype(BF16)
    const = lambda *shape: pl.BlockSpec(shape, lambda bi, ti: (0,) * len(shape),
                                        pipeline_mode=pl.Buffered(1))
    return pl.pallas_call(
        _gla_mixer_kernel,
        grid=(b, s // ts),
        in_specs=[
            pl.BlockSpec((1, ts, d), lambda bi, ti: (bi, ti, 0)),
            const(1, d),
            const(d, c_qkv),
            const(d, LANES),
            const(d, GLA_VAL_DIM),
            const(LANES, GLA_KEY_DIM),
            const(1, GLA_KEY_DIM),
            const(GLA_TRIL_ROWS, GLA_TRIL_ROWS),
            const(1, GLA_DV),
            const(GLA_VAL_DIM, d),
        ],
        out_specs=pl.BlockSpec((1, ts, d), lambda bi, ti: (bi, ti, 0)),
        out_shape=jax.ShapeDtypeStruct(x.shape, x.dtype),
        scratch_shapes=[
            pltpu.VMEM((GLA_HEADS, GLA_DK, GLA_DV), F32),
            pltpu.VMEM((ts // GLA_CHUNK, GLA_HEADS, GLA_DK, GLA_DV), F32),
            pltpu.VMEM((ts // GLA_CHUNK, GLA_HEADS, GLA_CHUNK, GLA_CHUNK), BF16),
            pltpu.VMEM((ts, GLA_VAL_DIM), F32),
        ],
        compiler_params=pltpu.CompilerParams(
            dimension_semantics=("arbitrary", "arbitrary"), vmem_limit_bytes=VMEM_LIMIT),
        name="gla_mixer",
    )(x, norm[None, :], w_qkv, w_glr, w_og, w_gup, gate_bias[None, :], tril, o_norm[None, :],
      w_out.astype(BF16))


def _ffn_ple_kernel(x_ref, p_ref, fnorm_ref, w_up_ref, conv_w_ref, conv_b_ref, w_down_ref, pnorm_ref,
                    w_gate_ref, w_proj_ref, o_ref, s0_tail, s1_tail, act_buf):
    ti = pl.program_id(1)
    ts = x_ref.shape[1]

    @pl.when(ti == 0)
    def _():
        s0_tail[...] = jnp.zeros_like(s0_tail)
        s1_tail[...] = jnp.zeros_like(s1_tail)

    rb = FFN_ROW_BLOCK
    n_chunks = D_FF // FF_CHUNK

    def conv_cols(u, c0):
        cols = slice(c0, c0 + FF_CHUNK)
        s0 = conv_w_ref[0:1, cols] * u
        s1 = conv_w_ref[1:2, cols] * u + _shift_rows(s0, s0_tail[:, cols], 1)
        out = conv_w_ref[2:3, cols] * u + _shift_rows(s1, s1_tail[:, cols], 1) + conv_b_ref[:, cols]
        s0_tail[:, cols] = s0[rb - SUBLANES:]
        s1_tail[:, cols] = s1[rb - SUBLANES:]
        return out

    def up_dots(h, ci):
        return (_dot(h, w_up_ref[:, ci * FF_CHUNK:(ci + 1) * FF_CHUNK]),
                _dot(h, w_up_ref[:, D_FF + ci * FF_CHUNK:D_FF + (ci + 1) * FF_CHUNK]))

    def embed(r0, x1):
        hp = _rms_norm(x1, pnorm_ref[...]).astype(BF16)
        gate = jax.nn.sigmoid(_dot(hp, w_gate_ref[...]))
        emb = _dot(p_ref[0, r0:r0 + rb, :].astype(BF16), w_proj_ref[...])
        o_ref[0, r0:r0 + rb, :] = x1 + gate * emb

    prev = None
    for r0 in range(0, ts, rb):
        h = _rms_norm(x_ref[0, r0:r0 + rb, :], fnorm_ref[...]).astype(BF16)
        nxt = up_dots(h, 0)
        acc = None
        for ci in range(n_chunks):
            cols = slice(ci * FF_CHUNK, (ci + 1) * FF_CHUNK)
            u_gate, u_up = nxt
            if ci + 1 < n_chunks:
                nxt = up_dots(h, ci + 1)
            if prev is not None:
                d = _dot(act_buf[prev:prev + rb, cols], w_down_ref[cols, :])
                acc = d if acc is None else acc + d
            gate = conv_cols(u_gate, ci * FF_CHUNK)
            up = conv_cols(u_up, D_FF + ci * FF_CHUNK)
            act_buf[r0:r0 + rb, cols] = (gate * (1.0 + lax.erf(gate)) * up).astype(BF16)
        if prev is not None:
            embed(prev, x_ref[0, prev:prev + rb, :] + acc)
        prev = r0
    embed(prev, x_ref[0, prev:prev + rb, :] + _dot(act_buf[prev:prev + rb, :], w_down_ref[...]))


def _ffn_ple(x, p, layer, fnorm, w_up, conv_w, conv_b, w_down, pnorm, w_gate, w_proj):
    b, s, d = x.shape
    ts = FFN_SEQ_TILE
    col_scale = jnp.where(jnp.arange(2 * D_FF) < D_FF, 2.0 ** -0.5, 1.0).astype(F32)[None, :]
    const = lambda *shape: pl.BlockSpec(shape, lambda bi, ti: (0,) * len(shape),
                                        pipeline_mode=pl.Buffered(1))
    return pl.pallas_call(
        _ffn_ple_kernel,
        grid=(b, s // ts),
        in_specs=[
            pl.BlockSpec((1, ts, d), lambda bi, ti: (bi, ti, 0)),
            pl.BlockSpec((None, 1, ts, PLE_DIM), lambda bi, ti: (layer, bi, ti, 0)),
            const(1, d),
            const(d, 2 * D_FF),
            const(3, 2 * D_FF),
            const(1, 2 * D_FF),
            const(D_FF, d),
            const(1, d),
            const(d, d),
            const(PLE_DIM, d),
        ],
        out_specs=pl.BlockSpec((1, ts, d), lambda bi, ti: (bi, ti, 0)),
        out_shape=jax.ShapeDtypeStruct(x.shape, x.dtype),
        scratch_shapes=[pltpu.VMEM((SUBLANES, 2 * D_FF), F32), pltpu.VMEM((SUBLANES, 2 * D_FF), F32),
                        pltpu.VMEM((ts, D_FF), BF16)],
        compiler_params=pltpu.CompilerParams(
            dimension_semantics=("arbitrary", "arbitrary"), vmem_limit_bytes=VMEM_LIMIT),
        name="ffn_ple",
    )(x, p, fnorm[None, :], (w_up * col_scale).astype(BF16), conv_w, (conv_b * col_scale[0])[None, :],
      (w_down * (0.5 * 2.0 ** 0.5)).astype(BF16),
      pnorm[None, :], w_gate.astype(BF16), w_proj.astype(BF16))


def kernel(x, p, positions, mix_norm, ffn_norm, ffn_w_up, ffn_conv_w, ffn_conv_b, ffn_w_down, ple_norm,
           ple_w_gate, ple_w_proj, hy_w_in, hy_q_norm, hy_k_norm, hy_sinks, hy_conv_w, hy_w_out, gla_w_in,
           gla_w_gate_up, gla_gate_bias, gla_o_norm, gla_w_out):
    depth = mix_norm.shape[0]
    for i in range(depth):
        j = i // 2
        if i % 2 == 0:
            x = _hy_mixer(x, positions, mix_norm[i], hy_w_in[j], hy_q_norm[j], hy_k_norm[j], hy_sinks[j],
                          hy_conv_w[j], hy_w_out[j])
        else:
            x = _gla_mixer(x, mix_norm[i], gla_w_in[j], gla_w_gate_up[j], gla_gate_bias[j], gla_o_norm[j],
                           gla_w_out[j])
        x = _ffn_ple(x, p, i, ffn_norm[i], ffn_w_up[i], ffn_conv_w[i], ffn_conv_b[i], ffn_w_down[i],
                     ple_norm[i], ple_w_gate[i], ple_w_proj[i])
    return x
```

```python
import jax
import jax.numpy as jnp
from jax import lax
from jax.experimental import pallas as pl
from jax.experimental.pallas import tpu as pltpu

D_MODEL = 1024
HEAD_DIM = 64
ATTN_Q_HEADS = 8
ATTN_KV_HEADS = 2
ATTN_BLOCK = 128
ROPE_THETA = 10000.0
ATTN_WIDTH = ATTN_Q_HEADS * HEAD_DIM
KV_WIDTH = ATTN_KV_HEADS * HEAD_DIM
CONV_WIDTH = D_MODEL // 2
HY_IN_WIDTH = ATTN_WIDTH + 2 * KV_WIDTH + 3 * CONV_WIDTH
GLA_HEADS = 4
GLA_KEY_DIM = D_MODEL // 2
GLA_VAL_DIM = D_MODEL
GLA_DK = GLA_KEY_DIM // GLA_HEADS
GLA_DV = GLA_VAL_DIM // GLA_HEADS
GLA_GATE_RANK = 16
GLA_GATE_NORMALIZER = 16.0
GLA_CHUNK = 64
GLA_TRIL_ROWS = 256
GLA_SUB = 16
D_FF = 2816
PLE_DIM = 256
RMS_EPS = 1e-6

LANES = 128
SUBLANES = 8
SEQ_TILE = 1024
FFN_SEQ_TILE = 512
FFN_ROW_BLOCK = 256
FF_CHUNK = 256
VMEM_LIMIT = 56 * 1024 * 1024
NEG_BIG = -1e30
LOG2_E = 1.4426950408889634

F32 = jnp.float32
BF16 = jnp.bfloat16


def _dot(a, b):
    return jnp.dot(a, b, preferred_element_type=F32)


def _dot_nt(a, b):
    return lax.dot_general(a, b, (((1,), (1,)), ((), ())), preferred_element_type=F32)


def _dot_tn(a, b):
    return lax.dot_general(a, b, (((0,), (0,)), ((), ())), preferred_element_type=F32)


def _rms_norm(x, gain):
    return x * lax.rsqrt(jnp.mean(x * x, axis=-1, keepdims=True) + RMS_EPS) * gain


def _shift_rows(u, tail, k):
    n = u.shape[0]
    ext = jnp.concatenate([tail, u], axis=0)
    return ext[SUBLANES - k:SUBLANES - k + n]


def _hy_mixer_kernel(x_ref, pos_ref, invf_ref, norm_ref, w_in_ref, qn_ref, kn_ref, sink_ref, hsum_ref,
                     conv_w_ref, w_out_ref, o_ref, k_tail, v_tail, y_tail, attn_buf):
    ti = pl.program_id(1)
    ts = x_ref.shape[1]

    @pl.when(ti == 0)
    def _():
        k_tail[...] = jnp.zeros_like(k_tail)
        v_tail[...] = jnp.zeros_like(v_tail)
        y_tail[...] = jnp.zeros_like(y_tail)

    x = x_ref[0]
    h = _rms_norm(x, norm_ref[...]).astype(BF16)

    pos = pos_ref[0].astype(F32)
    n_freq = HEAD_DIM // 2
    ang = invf_ref[0:n_freq, 0:1] * pos
    cos_t = jnp.concatenate([jnp.cos(ang)] * (LANES // n_freq), axis=0).T
    sin_t = jnp.concatenate([jnp.sin(ang)] * (LANES // n_freq), axis=0).T
    lane = lax.broadcasted_iota(jnp.int32, (1, LANES), 1)
    first_half = (lane % HEAD_DIM) < (HEAD_DIM // 2)
    sin_t = jnp.where(first_half, -sin_t, sin_t)

    def qk_norm_rope(t, gain_ref):
        w = t.shape[1]
        msq = _dot((t * t).astype(BF16), hsum_ref[0:w, 0:w])
        t = t * lax.rsqrt(msq + RMS_EPS) * gain_ref[...]
        reps = w // LANES
        cos_w = jnp.concatenate([cos_t] * reps, axis=1) if reps > 1 else cos_t
        sin_w = jnp.concatenate([sin_t] * reps, axis=1) if reps > 1 else sin_t
        fh = jnp.concatenate([first_half] * reps, axis=1) if reps > 1 else first_half
        half = HEAD_DIM // 2
        rot = jnp.where(fh, pltpu.roll(t, w - half, axis=1), pltpu.roll(t, half, axis=1))
        return t * cos_w + rot * sin_w

    q = _dot(h, w_in_ref[:, 0:ATTN_WIDTH])
    q = qk_norm_rope(q, qn_ref) * (HEAD_DIM ** -0.5 * LOG2_E)
    k = _dot(h, w_in_ref[:, ATTN_WIDTH:ATTN_WIDTH + KV_WIDTH])
    k = qk_norm_rope(k, kn_ref)
    v = _dot(h, w_in_ref[:, ATTN_WIDTH + KV_WIDTH:ATTN_WIDTH + 2 * KV_WIDTH])

    k_ext = jnp.concatenate([k_tail[...], k], axis=0)
    v_ext = jnp.concatenate([v_tail[...], v], axis=0)
    k_tail[...] = k[ts - ATTN_BLOCK:]
    v_tail[...] = v[ts - ATTN_BLOCK:]

    low = lane < HEAD_DIM
    k_sw = pltpu.roll(k_ext, HEAD_DIM, axis=1)
    v_sw = pltpu.roll(v_ext, HEAD_DIM, axis=1)
    zero = jnp.zeros_like(k_ext)
    k_lo = [jnp.where(low, k_ext, zero).astype(BF16), jnp.where(low, k_sw, zero).astype(BF16)]
    k_hi = [jnp.where(low, zero, k_sw).astype(BF16), jnp.where(low, zero, k_ext).astype(BF16)]
    v_lo = [jnp.where(low, v_ext, zero).astype(BF16), jnp.where(low, v_sw, zero).astype(BF16)]
    v_hi = [jnp.where(low, zero, v_sw).astype(BF16), jnp.where(low, zero, v_ext).astype(BF16)]

    n_blocks = ts // ATTN_BLOCK
    pair = 2 * ATTN_BLOCK
    qi = lax.broadcasted_iota(jnp.int32, (pair, pair), 0) % ATTN_BLOCK
    kj = lax.broadcasted_iota(jnp.int32, (pair, pair), 1)
    band = (kj > qi) & (kj <= qi + ATTN_BLOCK)
    band_first = band & (kj >= jnp.where(ti > 0, 0, ATTN_BLOCK))
    upper_rows = lax.broadcasted_iota(jnp.int32, (pair, 1), 0) >= ATTN_BLOCK

    q_bf = q.astype(BF16)
    units = [(jb, g) for jb in range(n_blocks) for g in range(ATTN_KV_HEADS)]

    def scores(jb, g):
        r0 = jb * ATTN_BLOCK
        qs = jnp.concatenate([q_bf[r0:r0 + ATTN_BLOCK, (2 * g + i) * LANES:(2 * g + i + 1) * LANES]
                              for i in range(2)], axis=0)
        kk = jnp.concatenate([k_lo[g][r0:r0 + pair], k_hi[g][r0:r0 + pair]], axis=0)
        return _dot_nt(qs, kk)

    c0 = ATTN_WIDTH + 2 * KV_WIDTH
    conv_starts = [c0 + i * 2 * LANES for i in range(3 * CONV_WIDTH // (2 * LANES))]
    conv_parts = []
    s_next = scores(*units[0])
    for ui, (jb, g) in enumerate(units):
        r0 = jb * ATTN_BLOCK
        s_cur = s_next
        if ui + 1 < len(units):
            s_next = scores(*units[ui + 1])
        if conv_starts:
            cs = conv_starts.pop(0)
            conv_parts.append(_dot(h, w_in_ref[:, cs:cs + 2 * LANES]))
        mask = band_first if jb == 0 else band
        probs, invs = [], []
        for half in range(2):
            s = jnp.where(mask, s_cur[:, half * pair:(half + 1) * pair], NEG_BIG)
            sk = jnp.where(upper_rows, sink_ref[4 * g + 2 + half], sink_ref[4 * g + half]) * LOG2_E
            m = jnp.maximum(jnp.max(s, axis=-1, keepdims=True), sk)
            p = jnp.exp2(s - m)
            invs.append(1.0 / (jnp.sum(p, axis=-1, keepdims=True) + jnp.exp2(sk - m)))
            probs.append(p.astype(BF16))
        vv = jnp.concatenate([v_lo[g][r0:r0 + pair], v_hi[g][r0:r0 + pair]], axis=0)
        o = _dot(jnp.concatenate(probs, axis=1), vv)
        o = o * jnp.where(low, invs[0], invs[1])
        for i in range(2):
            attn_buf[r0:r0 + ATTN_BLOCK, (2 * g + i) * LANES:(2 * g + i + 1) * LANES] = (
                o[i * ATTN_BLOCK:(i + 1) * ATTN_BLOCK].astype(BF16))
    cb = jnp.concatenate(conv_parts[0:2], axis=1)
    cc = jnp.concatenate(conv_parts[2:4], axis=1)
    cx = jnp.concatenate(conv_parts[4:6], axis=1)

    y = cc * cx
    tail = y_tail[...]
    conv = (conv_w_ref[2:3, :] * y + conv_w_ref[1:2, :] * _shift_rows(y, tail, 1)
            + conv_w_ref[0:1, :] * _shift_rows(y, tail, 2))
    y_tail[...] = y[ts - SUBLANES:]
    gated = (cb * conv).astype(BF16)

    out = _dot(attn_buf[...], w_out_ref[0:ATTN_WIDTH, :]) + _dot(gated, w_out_ref[ATTN_WIDTH:, :])
    o_ref[0] = x + out


def _hy_mixer(x, positions, norm, w_in, q_norm, k_norm, sinks, conv_w, w_out):
    b, s, d = x.shape
    ts = SEQ_TILE
    inv_freq = ROPE_THETA ** (-jnp.arange(0, HEAD_DIM, 2, dtype=F32) / HEAD_DIM)
    invf = jnp.broadcast_to(jnp.tile(inv_freq, LANES // (HEAD_DIM // 2))[:, None], (LANES, LANES))
    head_id = jnp.arange(ATTN_WIDTH) // HEAD_DIM
    hsum = jnp.where(head_id[:, None] == head_id[None, :], 1.0 / HEAD_DIM, 0.0).astype(BF16)
    qn = jnp.tile(q_norm, ATTN_Q_HEADS)[None, :]
    kn = jnp.tile(k_norm, ATTN_KV_HEADS)[None, :]
    const = lambda *shape: pl.BlockSpec(shape, lambda bi, ti: (0,) * len(shape),
                                        pipeline_mode=pl.Buffered(1))
    return pl.pallas_call(
        _hy_mixer_kernel,
        grid=(b, s // ts),
        in_specs=[
            pl.BlockSpec((1, ts, d), lambda bi, ti: (bi, ti, 0)),
            pl.BlockSpec((1, 1, ts), lambda bi, ti: (bi, 0, ti)),
            const(LANES, LANES),
            const(1, d),
            const(d, HY_IN_WIDTH),
            const(1, ATTN_WIDTH),
            const(1, KV_WIDTH),
            pl.BlockSpec(memory_space=pltpu.SMEM),
            const(ATTN_WIDTH, ATTN_WIDTH),
            const(3, CONV_WIDTH),
            const(d, d),
        ],
        out_specs=pl.BlockSpec((1, ts, d), lambda bi, ti: (bi, ti, 0)),
        out_shape=jax.ShapeDtypeStruct(x.shape, x.dtype),
        scratch_shapes=[
            pltpu.VMEM((ATTN_BLOCK, KV_WIDTH), F32),
            pltpu.VMEM((ATTN_BLOCK, KV_WIDTH), F32),
            pltpu.VMEM((SUBLANES, CONV_WIDTH), F32),
            pltpu.VMEM((ts, ATTN_WIDTH), BF16),
        ],
        compiler_params=pltpu.CompilerParams(
            dimension_semantics=("arbitrary", "arbitrary"), vmem_limit_bytes=VMEM_LIMIT),
        name="hy_mixer",
    )(x, positions.reshape(b, 1, s), invf, norm[None, :], w_in.astype(BF16), qn, kn, sinks, hsum,
      conv_w, w_out.astype(BF16))


def _gla_mixer_kernel(x_ref, norm_ref, w_qkv_ref, w_glr_ref, w_og_ref, w_gup_ref, gbias_ref, tril_ref,
                      onorm_ref, w_out_ref, o_ref, state, u_buf, sc_buf, o_buf):
    ti = pl.program_id(1)
    ts = x_ref.shape[1]
    c = GLA_CHUNK
    nc = ts // c
    nsub = c // GLA_SUB
    mid = GLA_SUB // 2
    kd = GLA_KEY_DIM

    @pl.when(ti == 0)
    def _():
        state[...] = jnp.zeros_like(state)

    x = x_ref[0]
    h = _rms_norm(x, norm_ref[...]).astype(BF16)

    g_lr = _dot(h, w_glr_ref[...])
    q_all = _dot(h, w_qkv_ref[:, 0:kd]) * (GLA_DK ** -0.5)
    z = _dot(g_lr.astype(BF16), w_gup_ref[...]) + gbias_ref[...]
    k_all = _dot(h, w_qkv_ref[:, kd:2 * kd])
    og = _dot(h, w_og_ref[...])
    log_a = (jnp.minimum(z, 0.0) - jnp.log(1.0 + jnp.exp(-jnp.abs(z)))) * (LOG2_E / GLA_GATE_NORMALIZER)
    la_hi = log_a.astype(BF16)
    la_lo = (log_a - la_hi.astype(F32)).astype(BF16)
    tril = tril_ref[...]
    tb = tril.shape[0]
    gc = jnp.concatenate([_dot(tril, la_hi[r:r + tb]) + _dot(tril, la_lo[r:r + tb]) for r in range(0, ts, tb)],
                         axis=0)
    v_all = _dot(h, w_qkv_ref[:, 2 * kd:]).astype(BF16)
    gate = og * jax.nn.sigmoid(og)

    g3 = gc.reshape(nc, c, kd)
    q3 = q_all.reshape(nc, c, kd)
    k3 = k_all.reshape(nc, c, kd)
    g_end = g3[:, c - 1:c, :]
    qe = (q3 * jnp.exp2(g3)).astype(BF16)
    k_dec = (k3 * jnp.exp2(g_end - g3)).astype(BF16)
    e_end = jnp.exp2(g_end)
    g_refs = [g3[:, sb * GLA_SUB + mid:sb * GLA_SUB + mid + 1, :] for sb in range(nsub)]
    g_own = jnp.concatenate([jnp.broadcast_to(r, (nc, GLA_SUB, kd)) for r in g_refs], axis=1)
    qt = (q3 * jnp.exp2(g3 - g_own)).astype(BF16)
    kts = []
    for sb in range(nsub):
        n = (sb + 1) * GLA_SUB
        kts.append((k3[:, 0:n, :] * jnp.exp2(g_refs[sb] - g3[:, 0:n, :])).astype(BF16))

    ri = lax.broadcasted_iota(jnp.int32, (c, c), 0)
    cj = lax.broadcasted_iota(jnp.int32, (c, c), 1)
    causal = cj <= ri
    zq = jnp.zeros((GLA_SUB, GLA_DK), BF16)

    e_cols = e_end.reshape(nc, kd).T

    def local_dots(ci):
        r0 = ci * c
        for hd in range(GLA_HEADS):
            ks = slice(hd * GLA_DK, (hd + 1) * GLA_DK)
            vh = v_all[r0:r0 + c, hd * GLA_DV:(hd + 1) * GLA_DV]
            q_cols, k_cols = [], []
            for sb in range(nsub):
                s0 = sb * GLA_SUB
                n = s0 + GLA_SUB
                q_cols.append(jnp.concatenate([zq] * sb + [qt[ci, s0:n, ks]] + [zq] * (nsub - 1 - sb), axis=0))
                k_cols.append(jnp.concatenate([kts[sb][ci, :, ks]] + [zq] * (nsub - 1 - sb), axis=0))
            sc = _dot_nt(jnp.concatenate(q_cols, axis=1), jnp.concatenate(k_cols, axis=1))
            sc_buf[ci, hd] = jnp.where(causal, sc, 0.0).astype(BF16)
            u_buf[ci, hd] = _dot_tn(k_dec[ci, :, ks], vh)

    sts = [state[hd] for hd in range(GLA_HEADS)]
    local_dots(0)
    for ci in range(nc):
        r0 = ci * c
        if ci + 1 < nc:
            local_dots(ci + 1)
        for hd in range(GLA_HEADS):
            ks = slice(hd * GLA_DK, (hd + 1) * GLA_DK)
            vh = v_all[r0:r0 + c, hd * GLA_DV:(hd + 1) * GLA_DV]
            lhs = jnp.concatenate([qe[ci, :, ks], sc_buf[ci, hd]], axis=1)
            rhs = jnp.concatenate([sts[hd].astype(BF16), vh], axis=0)
            o_buf[r0:r0 + c, hd * GLA_DV:(hd + 1) * GLA_DV] = _dot(lhs, rhs)
            sts[hd] = e_cols[hd * GLA_DK:(hd + 1) * GLA_DK, ci:ci + 1] * sts[hd] + u_buf[ci, hd]
    for hd in range(GLA_HEADS):
        state[hd] = sts[hd]
    parts = []
    for hd in range(GLA_HEADS):
        vs = slice(hd * GLA_DV, (hd + 1) * GLA_DV)
        o = o_buf[:, vs]
        o = o * lax.rsqrt(jnp.mean(o * o, axis=-1, keepdims=True) + RMS_EPS) * onorm_ref[...]
        parts.append((o * gate[:, vs]).astype(BF16))
    o_ref[0] = x + _dot(jnp.concatenate(parts, axis=1), w_out_ref[...])


def _gla_mixer(x, norm, w_in, w_gate_up, gate_bias, o_norm, w_out):
    b, s, d = x.shape
    ts = SEQ_TILE
    c_qkv = 2 * GLA_KEY_DIM + GLA_VAL_DIM
    w_qkv = w_in[:, :c_qkv].astype(BF16)
    w_glr = jnp.pad(w_in[:, c_qkv:c_qkv + GLA_GATE_RANK], ((0, 0), (0, LANES - GLA_GATE_RANK))).astype(BF16)
    w_og = w_in[:, c_qkv + GLA_GATE_RANK:].astype(BF16)
    w_gup = jnp.pad(w_gate_up, ((0, LANES - GLA_GATE_RANK), (0, 0))).astype(BF16)
    rows = jnp.arange(GLA_TRIL_ROWS)
    tril = ((rows[:, None] >= rows[None, :]) & (rows[:, None] // GLA_CHUNK == rows[None, :] // GLA_CHUNK)).astype(BF16)
    const = lambda *shape: pl.BlockSpec(shape, lambda bi, ti: (0,) * len(shape),
                                        pipeline_mode=pl.Buffered(1))
    return pl.pallas_call(
        _gla_mixer_kernel,
        grid=(b, s // ts),
        in_specs=[
            pl.BlockSpec((1, ts, d), lambda bi, ti: (bi, ti, 0)),
            const(1, d),
            const(d, c_qkv),
            const(d, LANES),
            const(d, GLA_VAL_DIM),
            const(LANES, GLA_KEY_DIM),
            const(1, GLA_KEY_DIM),
            const(GLA_TRIL_ROWS, GLA_TRIL_ROWS),
            const(1, GLA_DV),
            const(GLA_VAL_DIM, d),
        ],
        out_specs=pl.BlockSpec((1, ts, d), lambda bi, ti: (bi, ti, 0)),
        out_shape=jax.ShapeDtypeStruct(x.shape, x.dtype),
        scratch_shapes=[
            pltpu.VMEM((GLA_HEADS, GLA_DK, GLA_DV), F32),
            pltpu.VMEM((ts // GLA_CHUNK, GLA_HEADS, GLA_DK, GLA_DV), F32),
            pltpu.VMEM((ts // GLA_CHUNK, GLA_HEADS, GLA_CHUNK, GLA_CHUNK), BF16),
            pltpu.VMEM((ts, GLA_VAL_DIM), F32),
        ],
        compiler_params=pltpu.CompilerParams(
            dimension_semantics=("arbitrary", "arbitrary"), vmem_limit_bytes=VMEM_LIMIT),
        name="gla_mixer",
    )(x, norm[None, :], w_qkv, w_glr, w_og, w_gup, gate_bias[None, :], tril, o_norm[None, :],
      w_out.astype(BF16))


def _ffn_ple_kernel(x_ref, p_ref, fnorm_ref, w_up_ref, conv_w_ref, conv_b_ref, w_down_ref, pnorm_ref,
                    w_gate_ref, w_proj_ref, o_ref, u_tail, act_buf):
    ti = pl.program_id(1)
    ts = x_ref.shape[1]

    @pl.when(ti == 0)
    def _():
        u_tail[...] = jnp.zeros_like(u_tail)

    rb = FFN_ROW_BLOCK
    n_chunks = D_FF // FF_CHUNK

    def conv_cols(u, c0):
        cols = slice(c0, c0 + FF_CHUNK)
        u1 = _shift_rows(u, u_tail[:, cols], 1)
        u2 = _shift_rows(u, u_tail[:, cols], 2)
        u_tail[:, cols] = u[rb - SUBLANES:]
        return (conv_w_ref[2:3, cols] * u.astype(BF16) + conv_w_ref[1:2, cols] * u1.astype(BF16)
                + conv_w_ref[0:1, cols] * u2.astype(BF16) + conv_b_ref[:, cols])

    def up_dots(h, ci):
        return (_dot(h, w_up_ref[:, ci * FF_CHUNK:(ci + 1) * FF_CHUNK]),
                _dot(h, w_up_ref[:, D_FF + ci * FF_CHUNK:D_FF + (ci + 1) * FF_CHUNK]))

    def embed(r0, x1):
        hp = _rms_norm(x1, pnorm_ref[...]).astype(BF16)
        gate = jax.nn.sigmoid(_dot(hp, w_gate_ref[...]))
        emb = _dot(p_ref[0, r0:r0 + rb, :].astype(BF16), w_proj_ref[...])
        o_ref[0, r0:r0 + rb, :] = x1 + gate * emb

    prev = None
    for r0 in range(0, ts, rb):
        h = _rms_norm(x_ref[0, r0:r0 + rb, :], fnorm_ref[...]).astype(BF16)
        nxt = up_dots(h, 0)
        acc = None
        for ci in range(n_chunks):
            cols = slice(ci * FF_CHUNK, (ci + 1) * FF_CHUNK)
            u_gate, u_up = nxt
            if ci + 1 < n_chunks:
                nxt = up_dots(h, ci + 1)
            if prev is not None:
                d = _dot(act_buf[prev:prev + rb, cols], w_down_ref[cols, :])
                acc = d if acc is None else acc + d
            gate = conv_cols(u_gate, ci * FF_CHUNK)
            up = conv_cols(u_up, D_FF + ci * FF_CHUNK)
            act_buf[r0:r0 + rb, cols] = gate * (1.0 + lax.erf(gate)) * up
        if prev is not None:
            embed(prev, x_ref[0, prev:prev + rb, :] + acc)
        prev = r0
    embed(prev, x_ref[0, prev:prev + rb, :] + _dot(act_buf[prev:prev + rb, :], w_down_ref[...]))


def _ffn_ple(x, p, layer, fnorm, w_up, conv_w, conv_b, w_down, pnorm, w_gate, w_proj):
    b, s, d = x.shape
    ts = FFN_SEQ_TILE
    col_scale = jnp.where(jnp.arange(2 * D_FF) < D_FF, 2.0 ** -0.5, 1.0).astype(F32)[None, :]
    const = lambda *shape: pl.BlockSpec(shape, lambda bi, ti: (0,) * len(shape),
                                        pipeline_mode=pl.Buffered(1))
    return pl.pallas_call(
        _ffn_ple_kernel,
        grid=(b, s // ts),
        in_specs=[
            pl.BlockSpec((1, ts, d), lambda bi, ti: (bi, ti, 0)),
            pl.BlockSpec((None, 1, ts, PLE_DIM), lambda bi, ti: (layer, bi, ti, 0)),
            const(1, d),
            const(d, 2 * D_FF),
            const(3, 2 * D_FF),
            const(1, 2 * D_FF),
            const(D_FF, d),
            const(1, d),
            const(d, d),
            const(PLE_DIM, d),
        ],
        out_specs=pl.BlockSpec((1, ts, d), lambda bi, ti: (bi, ti, 0)),
        out_shape=jax.ShapeDtypeStruct(x.shape, x.dtype),
        scratch_shapes=[pltpu.VMEM((SUBLANES, 2 * D_FF), F32), pltpu.VMEM((ts, D_FF), BF16)],
        compiler_params=pltpu.CompilerParams(
            dimension_semantics=("arbitrary", "arbitrary"), vmem_limit_bytes=VMEM_LIMIT),
        name="ffn_ple",
    )(x, p, fnorm[None, :], (w_up * col_scale).astype(BF16), conv_w.astype(BF16),
      (conv_b * col_scale[0])[None, :].astype(BF16),
      (w_down * (0.5 * 2.0 ** 0.5)).astype(BF16),
      pnorm[None, :], w_gate.astype(BF16), w_proj.astype(BF16))


def kernel(x, p, positions, mix_norm, ffn_norm, ffn_w_up, ffn_conv_w, ffn_conv_b, ffn_w_down, ple_norm,
           ple_w_gate, ple_w_proj, hy_w_in, hy_q_norm, hy_k_norm, hy_sinks, hy_conv_w, hy_w_out, gla_w_in,
           gla_w_gate_up, gla_gate_bias, gla_o_norm, gla_w_out):
    depth = mix_norm.shape[0]
    for i in range(depth):
        j = i // 2
        if i % 2 == 0:
            x = _hy_mixer(x, positions, mix_norm[i], hy_w_in[j], hy_q_norm[j], hy_k_norm[j], hy_sinks[j],
                          hy_conv_w[j], hy_w_out[j])
        else:
            x = _gla_mixer(x, mix_norm[i], gla_w_in[j], gla_w_gate_up[j], gla_gate_bias[j], gla_o_norm[j],
                           gla_w_out[j])
        x = _ffn_ple(x, p, i, ffn_norm[i], ffn_w_up[i], ffn_conv_w[i], ffn_conv_b[i], ffn_w_down[i],
                     ple_norm[i], ple_w_gate[i], ple_w_proj[i])
    return x
```

```python
import jax
import jax.numpy as jnp
from jax import lax
from jax.experimental import pallas as pl
from jax.experimental.pallas import tpu as pltpu

D_MODEL = 1024
HEAD_DIM = 64
ATTN_Q_HEADS = 8
ATTN_KV_HEADS = 2
ATTN_BLOCK = 128
ROPE_THETA = 10000.0
ATTN_WIDTH = ATTN_Q_HEADS * HEAD_DIM
KV_WIDTH = ATTN_KV_HEADS * HEAD_DIM
CONV_WIDTH = D_MODEL // 2
HY_IN_WIDTH = ATTN_WIDTH + 2 * KV_WIDTH + 3 * CONV_WIDTH
GLA_HEADS = 4
GLA_KEY_DIM = D_MODEL // 2
GLA_VAL_DIM = D_MODEL
GLA_DK = GLA_KEY_DIM // GLA_HEADS
GLA_DV = GLA_VAL_DIM // GLA_HEADS
GLA_GATE_RANK = 16
GLA_GATE_NORMALIZER = 16.0
GLA_CHUNK = 64
GLA_TRIL_ROWS = 256
GLA_SUB = 16
D_FF = 2816
PLE_DIM = 256
RMS_EPS = 1e-6

LANES = 128
SUBLANES = 8
SEQ_TILE = 1024
FFN_SEQ_TILE = 512
FFN_ROW_BLOCK = 256
FF_CHUNK = 256
VMEM_LIMIT = 56 * 1024 * 1024
NEG_BIG = -1e30
LOG2_E = 1.4426950408889634

F32 = jnp.float32
BF16 = jnp.bfloat16


def _dot(a, b):
    return jnp.dot(a, b, preferred_element_type=F32)


def _dot_nt(a, b):
    return lax.dot_general(a, b, (((1,), (1,)), ((), ())), preferred_element_type=F32)


def _dot_tn(a, b):
    return lax.dot_general(a, b, (((0,), (0,)), ((), ())), preferred_element_type=F32)


def _rms_norm(x, gain):
    return x * lax.rsqrt(jnp.mean(x * x, axis=-1, keepdims=True) + RMS_EPS) * gain


def _shift_rows(u, tail, k):
    n = u.shape[0]
    ext = jnp.concatenate([tail, u], axis=0)
    return ext[SUBLANES - k:SUBLANES - k + n]


def _hy_mixer_kernel(x_ref, pos_ref, invf_ref, norm_ref, w_in_ref, qn_ref, kn_ref, sink_ref, hsum_ref,
                     conv_w_ref, w_out_ref, o_ref, k_tail, v_tail, y_tail, attn_buf):
    ti = pl.program_id(1)
    ts = x_ref.shape[1]

    @pl.when(ti == 0)
    def _():
        k_tail[...] = jnp.zeros_like(k_tail)
        v_tail[...] = jnp.zeros_like(v_tail)
        y_tail[...] = jnp.zeros_like(y_tail)

    x = x_ref[0]
    h = _rms_norm(x, norm_ref[...]).astype(BF16)

    pos = pos_ref[0].astype(F32)
    n_freq = HEAD_DIM // 2
    ang = invf_ref[0:n_freq, 0:1] * pos
    cos_t = jnp.concatenate([jnp.cos(ang)] * (LANES // n_freq), axis=0).T
    sin_t = jnp.concatenate([jnp.sin(ang)] * (LANES // n_freq), axis=0).T
    lane = lax.broadcasted_iota(jnp.int32, (1, LANES), 1)
    first_half = (lane % HEAD_DIM) < (HEAD_DIM // 2)
    sin_t = jnp.where(first_half, -sin_t, sin_t)

    def qk_norm_rope(t, gain_ref):
        w = t.shape[1]
        msq = _dot((t * t).astype(BF16), hsum_ref[0:w, 0:w])
        t = t * lax.rsqrt(msq + RMS_EPS) * gain_ref[...]
        reps = w // LANES
        cos_w = jnp.concatenate([cos_t] * reps, axis=1) if reps > 1 else cos_t
        sin_w = jnp.concatenate([sin_t] * reps, axis=1) if reps > 1 else sin_t
        fh = jnp.concatenate([first_half] * reps, axis=1) if reps > 1 else first_half
        half = HEAD_DIM // 2
        rot = jnp.where(fh, pltpu.roll(t, w - half, axis=1), pltpu.roll(t, half, axis=1))
        return t * cos_w + rot * sin_w

    q = _dot(h, w_in_ref[:, 0:ATTN_WIDTH])
    q = qk_norm_rope(q, qn_ref) * (HEAD_DIM ** -0.5 * LOG2_E)
    k = _dot(h, w_in_ref[:, ATTN_WIDTH:ATTN_WIDTH + KV_WIDTH])
    k = qk_norm_rope(k, kn_ref)
    v = _dot(h, w_in_ref[:, ATTN_WIDTH + KV_WIDTH:ATTN_WIDTH + 2 * KV_WIDTH])

    k_ext = jnp.concatenate([k_tail[...], k], axis=0)
    v_ext = jnp.concatenate([v_tail[...], v], axis=0)
    k_tail[...] = k[ts - ATTN_BLOCK:]
    v_tail[...] = v[ts - ATTN_BLOCK:]

    low = lane < HEAD_DIM
    k_sw = pltpu.roll(k_ext, HEAD_DIM, axis=1)
    v_sw = pltpu.roll(v_ext, HEAD_DIM, axis=1)
    zero = jnp.zeros_like(k_ext)
    k_lo = [jnp.where(low, k_ext, zero).astype(BF16), jnp.where(low, k_sw, zero).astype(BF16)]
    k_hi = [jnp.where(low, zero, k_sw).astype(BF16), jnp.where(low, zero, k_ext).astype(BF16)]
    v_lo = [jnp.where(low, v_ext, zero).astype(BF16), jnp.where(low, v_sw, zero).astype(BF16)]
    v_hi = [jnp.where(low, zero, v_sw).astype(BF16), jnp.where(low, zero, v_ext).astype(BF16)]

    n_blocks = ts // ATTN_BLOCK
    pair = 2 * ATTN_BLOCK
    qi = lax.broadcasted_iota(jnp.int32, (pair, pair), 0) % ATTN_BLOCK
    kj = lax.broadcasted_iota(jnp.int32, (pair, pair), 1)
    band = (kj > qi) & (kj <= qi + ATTN_BLOCK)
    band_first = band & (kj >= jnp.where(ti > 0, 0, ATTN_BLOCK))
    upper_rows = lax.broadcasted_iota(jnp.int32, (pair, 1), 0) >= ATTN_BLOCK

    q_bf = q.astype(BF16)
    units = [(jb, g) for jb in range(n_blocks) for g in range(ATTN_KV_HEADS)]

    def scores(jb, g):
        r0 = jb * ATTN_BLOCK
        qs = jnp.concatenate([q_bf[r0:r0 + ATTN_BLOCK, (2 * g + i) * LANES:(2 * g + i + 1) * LANES]
                              for i in range(2)], axis=0)
        kk = jnp.concatenate([k_lo[g][r0:r0 + pair], k_hi[g][r0:r0 + pair]], axis=0)
        return _dot_nt(qs, kk)

    c0 = ATTN_WIDTH + 2 * KV_WIDTH
    conv_starts = [c0 + i * 2 * LANES for i in range(3 * CONV_WIDTH // (2 * LANES))]
    conv_parts = []
    s_next = scores(*units[0])
    for ui, (jb, g) in enumerate(units):
        r0 = jb * ATTN_BLOCK
        s_cur = s_next
        if ui + 1 < len(units):
            s_next = scores(*units[ui + 1])
        if conv_starts:
            cs = conv_starts.pop(0)
            conv_parts.append(_dot(h, w_in_ref[:, cs:cs + 2 * LANES]))
        mask = band_first if jb == 0 else band
        probs, invs = [], []
        for half in range(2):
            s = jnp.where(mask, s_cur[:, half * pair:(half + 1) * pair], NEG_BIG)
            sk = jnp.where(upper_rows, sink_ref[4 * g + 2 + half], sink_ref[4 * g + half]) * LOG2_E
            m = jnp.maximum(jnp.max(s, axis=-1, keepdims=True), sk)
            p = jnp.exp2(s - m)
            invs.append(1.0 / (jnp.sum(p, axis=-1, keepdims=True) + jnp.exp2(sk - m)))
            probs.append(p.astype(BF16))
        vv = jnp.concatenate([v_lo[g][r0:r0 + pair], v_hi[g][r0:r0 + pair]], axis=0)
        o = _dot(jnp.concatenate(probs, axis=1), vv)
        o = o * jnp.where(low, invs[0], invs[1])
        for i in range(2):
            attn_buf[r0:r0 + ATTN_BLOCK, (2 * g + i) * LANES:(2 * g + i + 1) * LANES] = (
                o[i * ATTN_BLOCK:(i + 1) * ATTN_BLOCK].astype(BF16))
    cb = jnp.concatenate(conv_parts[0:2], axis=1)
    cc = jnp.concatenate(conv_parts[2:4], axis=1)
    cx = jnp.concatenate(conv_parts[4:6], axis=1)

    y = cc * cx
    tail = y_tail[...]
    conv = (conv_w_ref[2:3, :] * y + conv_w_ref[1:2, :] * _shift_rows(y, tail, 1)
            + conv_w_ref[0:1, :] * _shift_rows(y, tail, 2))
    y_tail[...] = y[ts - SUBLANES:]
    gated = (cb * conv).astype(BF16)

    out = _dot(attn_buf[...], w_out_ref[0:ATTN_WIDTH, :]) + _dot(gated, w_out_ref[ATTN_WIDTH:, :])
    o_ref[0] = x + out


def _hy_mixer(x, positions, norm, w_in, q_norm, k_norm, sinks, conv_w, w_out):
    b, s, d = x.shape
    ts = SEQ_TILE
    inv_freq = ROPE_THETA ** (-jnp.arange(0, HEAD_DIM, 2, dtype=F32) / HEAD_DIM)
    invf = jnp.broadcast_to(jnp.tile(inv_freq, LANES // (HEAD_DIM // 2))[:, None], (LANES, LANES))
    head_id = jnp.arange(ATTN_WIDTH) // HEAD_DIM
    hsum = jnp.where(head_id[:, None] == head_id[None, :], 1.0 / HEAD_DIM, 0.0).astype(BF16)
    qn = jnp.tile(q_norm, ATTN_Q_HEADS)[None, :]
    kn = jnp.tile(k_norm, ATTN_KV_HEADS)[None, :]
    const = lambda *shape: pl.BlockSpec(shape, lambda bi, ti: (0,) * len(shape),
                                        pipeline_mode=pl.Buffered(1))
    return pl.pallas_call(
        _hy_mixer_kernel,
        grid=(b, s // ts),
        in_specs=[
            pl.BlockSpec((1, ts, d), lambda bi, ti: (bi, ti, 0)),
            pl.BlockSpec((1, 1, ts), lambda bi, ti: (bi, 0, ti)),
            const(LANES, LANES),
            const(1, d),
            const(d, HY_IN_WIDTH),
            const(1, ATTN_WIDTH),
            const(1, KV_WIDTH),
            pl.BlockSpec(memory_space=pltpu.SMEM),
            const(ATTN_WIDTH, ATTN_WIDTH),
            const(3, CONV_WIDTH),
            const(d, d),
        ],
        out_specs=pl.BlockSpec((1, ts, d), lambda bi, ti: (bi, ti, 0)),
        out_shape=jax.ShapeDtypeStruct(x.shape, x.dtype),
        scratch_shapes=[
            pltpu.VMEM((ATTN_BLOCK, KV_WIDTH), F32),
            pltpu.VMEM((ATTN_BLOCK, KV_WIDTH), F32),
            pltpu.VMEM((SUBLANES, CONV_WIDTH), F32),
            pltpu.VMEM((ts, ATTN_WIDTH), BF16),
        ],
        compiler_params=pltpu.CompilerParams(
            dimension_semantics=("arbitrary", "arbitrary"), vmem_limit_bytes=VMEM_LIMIT),
        name="hy_mixer",
    )(x, positions.reshape(b, 1, s), invf, norm[None, :], w_in.astype(BF16), qn, kn, sinks, hsum,
      conv_w, w_out.astype(BF16))


def _gla_mixer_kernel(x_ref, norm_ref, w_qkv_ref, w_glr_ref, w_og_ref, w_gup_ref, gbias_ref, tril_ref,
                      onorm_ref, w_out_ref, o_ref, state, u_buf, sc_buf, o_buf):
    ti = pl.program_id(1)
    ts = x_ref.shape[1]
    c = GLA_CHUNK
    nc = ts // c
    nsub = c // GLA_SUB
    mid = GLA_SUB // 2
    kd = GLA_KEY_DIM

    @pl.when(ti == 0)
    def _():
        state[...] = jnp.zeros_like(state)

    x = x_ref[0]
    h = _rms_norm(x, norm_ref[...]).astype(BF16)

    g_lr = _dot(h, w_glr_ref[...])
    q_all = _dot(h, w_qkv_ref[:, 0:kd]) * (GLA_DK ** -0.5)
    z = _dot(g_lr.astype(BF16), w_gup_ref[...]) + gbias_ref[...]
    k_all = _dot(h, w_qkv_ref[:, kd:2 * kd])
    og = _dot(h, w_og_ref[...])
    log_a = (jnp.minimum(z, 0.0) - jnp.log(1.0 + jnp.exp(-jnp.abs(z)))) * (LOG2_E / GLA_GATE_NORMALIZER)
    la_hi = log_a.astype(BF16)
    la_lo = (log_a - la_hi.astype(F32)).astype(BF16)
    tril = tril_ref[...]
    tb = tril.shape[0]
    gc = jnp.concatenate([_dot(tril, la_hi[r:r + tb]) + _dot(tril, la_lo[r:r + tb]) for r in range(0, ts, tb)],
                         axis=0)
    v_all = _dot(h, w_qkv_ref[:, 2 * kd:]).astype(BF16)
    gate = og * jax.nn.sigmoid(og)

    g3 = gc.reshape(nc, c, kd)
    q3 = q_all.reshape(nc, c, kd)
    k3 = k_all.reshape(nc, c, kd)
    g_end = g3[:, c - 1:c, :]
    qe = (q3 * jnp.exp2(g3)).astype(BF16)
    k_dec = (k3 * jnp.exp2(g_end - g3)).astype(BF16)
    e_end = jnp.exp2(g_end)
    g_refs = [g3[:, sb * GLA_SUB + mid:sb * GLA_SUB + mid + 1, :] for sb in range(nsub)]
    g_own = jnp.concatenate([jnp.broadcast_to(r, (nc, GLA_SUB, kd)) for r in g_refs], axis=1)
    qt = (q3 * jnp.exp2(g3 - g_own)).astype(BF16)
    kts = []
    for sb in range(nsub):
        n = (sb + 1) * GLA_SUB
        kts.append((k3[:, 0:n, :] * jnp.exp2(g_refs[sb] - g3[:, 0:n, :])).astype(BF16))

    ri = lax.broadcasted_iota(jnp.int32, (c, c), 0)
    cj = lax.broadcasted_iota(jnp.int32, (c, c), 1)
    causal = cj <= ri
    zq = jnp.zeros((GLA_SUB, GLA_DK), BF16)

    e_cols = e_end.reshape(nc, kd).T

    def local_dots(ci):
        r0 = ci * c
        for hd in range(GLA_HEADS):
            ks = slice(hd * GLA_DK, (hd + 1) * GLA_DK)
            vh = v_all[r0:r0 + c, hd * GLA_DV:(hd + 1) * GLA_DV]
            q_cols, k_cols = [], []
            for sb in range(nsub):
                s0 = sb * GLA_SUB
                n = s0 + GLA_SUB
                q_cols.append(jnp.concatenate([zq] * sb + [qt[ci, s0:n, ks]] + [zq] * (nsub - 1 - sb), axis=0))
                k_cols.append(jnp.concatenate([kts[sb][ci, :, ks]] + [zq] * (nsub - 1 - sb), axis=0))
            sc = _dot_nt(jnp.concatenate(q_cols, axis=1), jnp.concatenate(k_cols, axis=1))
            sc_buf[ci, hd] = jnp.where(causal, sc, 0.0).astype(BF16)
            u_buf[ci, hd] = _dot_tn(k_dec[ci, :, ks], vh)

    sts = [state[hd] for hd in range(GLA_HEADS)]
    local_dots(0)
    for ci in range(nc):
        r0 = ci * c
        if ci + 1 < nc:
            local_dots(ci + 1)
        for hd in range(GLA_HEADS):
            ks = slice(hd * GLA_DK, (hd + 1) * GLA_DK)
            vh = v_all[r0:r0 + c, hd * GLA_DV:(hd + 1) * GLA_DV]
            lhs = jnp.concatenate([qe[ci, :, ks], sc_buf[ci, hd]], axis=1)
            rhs = jnp.concatenate([sts[hd].astype(BF16), vh], axis=0)
            o_buf[r0:r0 + c, hd * GLA_DV:(hd + 1) * GLA_DV] = _dot(lhs, rhs)
            sts[hd] = e_cols[hd * GLA_DK:(hd + 1) * GLA_DK, ci:ci + 1] * sts[hd] + u_buf[ci, hd]
    for hd in range(GLA_HEADS):
        state[hd] = sts[hd]
    parts = []
    for hd in range(GLA_HEADS):
        vs = slice(hd * GLA_DV, (hd + 1) * GLA_DV)
        o = o_buf[:, vs]
        o = o * lax.rsqrt(jnp.mean(o * o, axis=-1, keepdims=True) + RMS_EPS) * onorm_ref[...]
        parts.append((o * gate[:, vs]).astype(BF16))
    o_ref[0] = x + _dot(jnp.concatenate(parts, axis=1), w_out_ref[...])


def _gla_mixer(x, norm, w_in, w_gate_up, gate_bias, o_norm, w_out):
    b, s, d = x.shape
    ts = SEQ_TILE
    c_qkv = 2 * GLA_KEY_DIM + GLA_VAL_DIM
    w_qkv = w_in[:, :c_qkv].astype(BF16)
    w_glr = jnp.pad(w_in[:, c_qkv:c_qkv + GLA_GATE_RANK], ((0, 0), (0, LANES - GLA_GATE_RANK))).astype(BF16)
    w_og = w_in[:, c_qkv + GLA_GATE_RANK:].astype(BF16)
    w_gup = jnp.pad(w_gate_up, ((0, LANES - GLA_GATE_RANK), (0, 0))).astype(BF16)
    rows = jnp.arange(GLA_TRIL_ROWS)
    tril = ((rows[:, None] >= rows[None, :]) & (rows[:, None] // GLA_CHUNK == rows[None, :] // GLA_CHUNK)).astype(BF16)
    const = lambda *shape: pl.BlockSpec(shape, lambda bi, ti: (0,) * len(shape),
                                        pipeline_mode=pl.Buffered(1))
    return pl.pallas_call(
        _gla_mixer_kernel,
        grid=(b, s // ts),
        in_specs=[
            pl.BlockSpec((1, ts, d), lambda bi, ti: (bi, ti, 0)),
            const(1, d),
            const(d, c_qkv),
            const(d, LANES),
            const(d, GLA_VAL_DIM),
            const(LANES, GLA_KEY_DIM),
            const(1, GLA_KEY_DIM),
            const(GLA_TRIL_ROWS, GLA_TRIL_ROWS),
            const(1, GLA_DV),
            const(GLA_VAL_DIM, d),
        ],
        out_specs=pl.BlockSpec((1, ts, d), lambda bi, ti: (bi, ti, 0)),
        out_shape=jax.ShapeDtypeStruct(x.shape, x.dtype),
        scratch_shapes=[
            pltpu.VMEM((GLA_HEADS, GLA_DK, GLA_DV), F32),
            pltpu.VMEM((ts // GLA_CHUNK, GLA_HEADS, GLA_DK, GLA_DV), F32),
            pltpu.VMEM((ts // GLA_CHUNK, GLA_HEADS, GLA_CHUNK, GLA_CHUNK), BF16),
            pltpu.VMEM((ts, GLA_VAL_DIM), F32),
        ],
        compiler_params=pltpu.CompilerParams(
            dimension_semantics=("arbitrary", "arbitrary"), vmem_limit_bytes=VMEM_LIMIT),
        name="gla_mixer",
    )(x, norm[None, :], w_qkv, w_glr, w_og, w_gup, gate_bias[None, :], tril, o_norm[None, :],
      w_out.astype(BF16))


def _ffn_ple_kernel(x_ref, p_ref, fnorm_ref, w_up_ref, conv_w_ref, conv_b_ref, w_down_ref, pnorm_ref,
                    w_gate_ref, w_proj_ref, o_ref, u_tail, act_buf):
    ti = pl.program_id(1)
    ts = x_ref.shape[1]

    @pl.when(ti == 0)
    def _():
        u_tail[...] = jnp.zeros_like(u_tail)

    rb = FFN_ROW_BLOCK
    n_chunks = D_FF // FF_CHUNK

    def conv_cols(u, c0):
        cols = slice(c0, c0 + FF_CHUNK)
        u1 = _shift_rows(u, u_tail[:, cols], 1)
        u2 = _shift_rows(u, u_tail[:, cols], 2)
        u_tail[:, cols] = u[rb - SUBLANES:]
        return (conv_w_ref[2:3, cols] * u.astype(BF16) + conv_w_ref[1:2, cols] * u1.astype(BF16)
                + conv_w_ref[0:1, cols] * u2.astype(BF16) + conv_b_ref[:, cols])

    def up_dots(h, ci):
        return (_dot(h, w_up_ref[:, ci * FF_CHUNK:(ci + 1) * FF_CHUNK]),
                _dot(h, w_up_ref[:, D_FF + ci * FF_CHUNK:D_FF + (ci + 1) * FF_CHUNK]))

    embs = {r0: _dot(p_ref[0, r0:r0 + rb, :].astype(BF16), w_proj_ref[...]) for r0 in range(0, ts, rb)}

    def embed(r0, x1):
        hp = _rms_norm(x1, pnorm_ref[...]).astype(BF16)
        gate = jax.nn.sigmoid(_dot(hp, w_gate_ref[...]))
        o_ref[0, r0:r0 + rb, :] = x1 + gate * embs[r0]

    prev = None
    for r0 in range(0, ts, rb):
        h = _rms_norm(x_ref[0, r0:r0 + rb, :], fnorm_ref[...]).astype(BF16)
        nxt = up_dots(h, 0)
        acc = None
        for ci in range(n_chunks):
            cols = slice(ci * FF_CHUNK, (ci + 1) * FF_CHUNK)
            u_gate, u_up = nxt
            if ci + 1 < n_chunks:
                nxt = up_dots(h, ci + 1)
            if prev is not None:
                d = _dot(act_buf[prev:prev + rb, cols], w_down_ref[cols, :])
                acc = d if acc is None else acc + d
            gate = conv_cols(u_gate, ci * FF_CHUNK)
            up = conv_cols(u_up, D_FF + ci * FF_CHUNK)
            act_buf[r0:r0 + rb, cols] = gate * (1.0 + lax.erf(gate)) * up
        if prev is not None:
            embed(prev, x_ref[0, prev:prev + rb, :] + acc)
        prev = r0
    embed(prev, x_ref[0, prev:prev + rb, :] + _dot(act_buf[prev:prev + rb, :], w_down_ref[...]))


def _ffn_ple(x, p, layer, fnorm, w_up, conv_w, conv_b, w_down, pnorm, w_gate, w_proj):
    b, s, d = x.shape
    ts = FFN_SEQ_TILE
    col_scale = jnp.where(jnp.arange(2 * D_FF) < D_FF, 2.0 ** -0.5, 1.0).astype(F32)[None, :]
    const = lambda *shape: pl.BlockSpec(shape, lambda bi, ti: (0,) * len(shape),
                                        pipeline_mode=pl.Buffered(1))
    return pl.pallas_call(
        _ffn_ple_kernel,
        grid=(b, s // ts),
        in_specs=[
            pl.BlockSpec((1, ts, d), lambda bi, ti: (bi, ti, 0)),
            pl.BlockSpec((None, 1, ts, PLE_DIM), lambda bi, ti: (layer, bi, ti, 0)),
            const(1, d),
            const(d, 2 * D_FF),
            const(3, 2 * D_FF),
            const(1, 2 * D_FF),
            const(D_FF, d),
            const(1, d),
            const(d, d),
            const(PLE_DIM, d),
        ],
        out_specs=pl.BlockSpec((1, ts, d), lambda bi, ti: (bi, ti, 0)),
        out_shape=jax.ShapeDtypeStruct(x.shape, x.dtype),
        scratch_shapes=[pltpu.VMEM((SUBLANES, 2 * D_FF), F32), pltpu.VMEM((ts, D_FF), BF16)],
        compiler_params=pltpu.CompilerParams(
            dimension_semantics=("arbitrary", "arbitrary"), vmem_limit_bytes=VMEM_LIMIT),
        name="ffn_ple",
    )(x, p, fnorm[None, :], (w_up * col_scale).astype(BF16), conv_w.astype(BF16),
      (conv_b * col_scale[0])[None, :].astype(BF16),
      (w_down * (0.5 * 2.0 ** 0.5)).astype(BF16),
      pnorm[None, :], w_gate.astype(BF16), w_proj.astype(BF16))


def kernel(x, p, positions, mix_norm, ffn_norm, ffn_w_up, ffn_conv_w, ffn_conv_b, ffn_w_down, ple_norm,
           ple_w_gate, ple_w_proj, hy_w_in, hy_q_norm, hy_k_norm, hy_sinks, hy_conv_w, hy_w_out, gla_w_in,
           gla_w_gate_up, gla_gate_bias, gla_o_norm, gla_w_out):
    depth = mix_norm.shape[0]
    for i in range(depth):
        j = i // 2
        if i % 2 == 0:
            x = _hy_mixer(x, positions, mix_norm[i], hy_w_in[j], hy_q_norm[j], hy_k_norm[j], hy_sinks[j],
                          hy_conv_w[j], hy_w_out[j])
        else:
            x = _gla_mixer(x, mix_norm[i], gla_w_in[j], gla_w_gate_up[j], gla_gate_bias[j], gla_o_norm[j],
                           gla_w_out[j])
        x = _ffn_ple(x, p, i, ffn_norm[i], ffn_w_up[i], ffn_conv_w[i], ffn_conv_b[i], ffn_w_down[i],
                     ple_norm[i], ple_w_gate[i], ple_w_proj[i])
    return x
```
